```python
import jax, jax.numpy as jnp
from jax import lax
import numpy as np

D_MODEL = 1024
BATCH = 16
SEQ = 4096
DEPTH = 2
DEC_BATCH = 16
DEC_SEQ = 16
PAST_LEN = 4096

CHUNK = 64
CONV_DIM = 512
CONV_WIDTH = 3
MLSTM_HEADS = 4
MLSTM_HEAD_DIM = 128
MLSTM_DIM = MLSTM_HEADS * MLSTM_HEAD_DIM
MIX_DIM = CONV_DIM + MLSTM_DIM
PROJ_DIM = 3 * CONV_DIM + 4 * MLSTM_DIM + 2 * MLSTM_HEADS
D_FF = -(-8 * D_MODEL // (3 * 256)) * 256
EPS = 1e-6

kernel_name = 'hybrid_shortconv_mlstm_stream_step'


def rms_norm(x, g):
    xf = x.astype(jnp.float32)
    y = xf * lax.rsqrt(jnp.mean(xf * xf, axis=-1, keepdims=True) + EPS)
    return (y * g.astype(jnp.float32)).astype(x.dtype)


def short_conv(u, gate_b, gate_c, conv_w, conv_state):
    T = u.shape[1]
    z = gate_c * u
    zp = jnp.concatenate([conv_state.astype(z.dtype), z], axis=1)
    y = zp[:, 0:T] * conv_w[0]
    for j in range(1, CONV_WIDTH):
        y = y + zp[:, j:j + T] * conv_w[j]
    return gate_b * y, zp[:, T:]


def _mlstm_chunk(carry, inp):
    C, n, m = carry
    q, k, v, li, lf = inp
    L = q.shape[2]
    b = jnp.cumsum(lf, axis=-1)
    causal = jnp.tril(jnp.ones((L, L), dtype=bool))
    log_w = jnp.where(causal, b[..., :, None] - b[..., None, :] + li[..., None, :], -jnp.inf)
    log_g = b + m[..., None]
    m_row = jnp.maximum(log_g, jnp.max(log_w, axis=-1))
    w_intra = jnp.exp(log_w - m_row[..., None])
    w_carry = jnp.exp(log_g - m_row)
    s = jnp.einsum('bhld,bhsd->bhls', q, k) * w_intra
    num = w_carry[..., None] * jnp.einsum('bhld,bhde->bhle', q, C) + jnp.einsum('bhls,bhse->bhle', s, v)
    den = w_carry * jnp.einsum('bhld,bhd->bhl', q, n) + jnp.sum(s, axis=-1)
    h = num / jnp.maximum(jnp.abs(den), jnp.exp(-m_row))[..., None]
    m_new = m_row[..., -1]
    decay = jnp.exp(b[..., -1] + m - m_new)
    w_s = jnp.exp(b[..., -1:] - b + li - m_new[..., None])
    C_new = decay[..., None, None] * C + jnp.einsum('bhs,bhsd,bhse->bhde', w_s, k, v)
    n_new = decay[..., None] * n + jnp.einsum('bhs,bhsd->bhd', w_s, k)
    return (C_new, n_new, m_new), h


def mlstm(q, k, v, li, lf, C0, n0, m0, chunk):
    B, T, H, Dh = q.shape
    nc = T // chunk

    def to_blocks(a):
        a = a.reshape((B, nc, chunk) + a.shape[2:])
        return jnp.moveaxis(a, (1, 3), (0, 2))

    k = k * (Dh ** -0.5)
    xs = (to_blocks(q), to_blocks(k), to_blocks(v), to_blocks(li), to_blocks(lf))
    (C, n, m), h = lax.scan(_mlstm_chunk, (C0, n0, m0), xs)
    h = jnp.moveaxis(h, (0, 2), (1, 3)).reshape(B, T, H, Dh)
    return h, C, n, m


def block(x, conv_state, C0, n0, m0, chunk, norm1_g, w_mix_in, conv_w, b_igate, b_fgate,
          head_norm_g, w_mix_out, norm2_g, w_gate, w_up, w_down):
    B, T, _ = x.shape
    f32 = jnp.float32
    h = rms_norm(x, norm1_g)
    p = jnp.einsum('btd,dp->btp', h, w_mix_in)
    sizes = (CONV_DIM,) * 3 + (MLSTM_DIM,) * 4 + (MLSTM_HEADS,) * 2
    idx = [int(i) for i in np.cumsum(sizes)[:-1]]
    u, gb, gc, q, k, v, o, ig, fg = jnp.split(p, idx, axis=-1)
    y_conv, conv_new = short_conv(u, gb, gc, conv_w, conv_state)
    hd = lambda a: a.reshape(B, T, MLSTM_HEADS, MLSTM_HEAD_DIM).astype(f32)
    li = ig.astype(f32) + b_igate.astype(f32)
    lf = jax.nn.log_sigmoid(fg.astype(f32) + b_fgate.astype(f32))
    h_cell, C, n, m = mlstm(hd(q), hd(k), hd(v), li, lf, C0, n0, m0, chunk)
    mu = jnp.mean(h_cell, axis=-1, keepdims=True)
    var = jnp.mean(jnp.square(h_cell - mu), axis=-1, keepdims=True)
    hn = (h_cell - mu) * lax.rsqrt(var + EPS) * head_norm_g.astype(f32)
    y_ml = (hn * jax.nn.sigmoid(hd(o))).reshape(B, T, MLSTM_DIM).astype(x.dtype)
    mix = jnp.einsum('btc,cd->btd', jnp.concatenate([y_conv, y_ml], axis=-1), w_mix_out)
    x = x + mix
    h2 = rms_norm(x, norm2_g)
    ff = jnp.einsum('btf,fd->btd',
                    jax.nn.silu(jnp.einsum('btd,df->btf', h2, w_gate)) * jnp.einsum('btd,df->btf', h2, w_up),
                    w_down)
    return x + ff, conv_new, C, n, m


def setup_inputs(seed: int = 0) -> dict:
    key = jax.random.key(seed)
    ks = jax.random.split(key, 20)
    nrm = lambda k, s: jax.random.normal(k, s, dtype=jnp.float32)
    return {
        'x_prompt': nrm(ks[0], (BATCH, SEQ, D_MODEL)),
        'x_sample': nrm(ks[1], (DEC_BATCH, DEC_SEQ, D_MODEL)),
        'state_conv': 0.5 * nrm(ks[2], (DEPTH, DEC_BATCH, CONV_WIDTH - 1, CONV_DIM)),
        'state_mlstm_C': nrm(ks[3], (DEPTH, DEC_BATCH, MLSTM_HEADS, MLSTM_HEAD_DIM, MLSTM_HEAD_DIM)),
        'state_mlstm_n': nrm(ks[4], (DEPTH, DEC_BATCH, MLSTM_HEADS, MLSTM_HEAD_DIM)),
        'state_mlstm_m': 0.5 * nrm(ks[5], (DEPTH, DEC_BATCH, MLSTM_HEADS)),
        'norm1_g': 1.0 + 0.01 * nrm(ks[6], (DEPTH, D_MODEL)),
        'w_mix_in': nrm(ks[7], (DEPTH, D_MODEL, PROJ_DIM)) * D_MODEL ** -0.5,
        'conv_w': nrm(ks[8], (DEPTH, CONV_WIDTH, CONV_DIM)) * CONV_WIDTH ** -0.5,
        'b_igate': 0.1 * nrm(ks[9], (DEPTH, MLSTM_HEADS)),
        'b_fgate': jnp.linspace(3.0, 6.0, MLSTM_HEADS, dtype=jnp.float32)[None, :] + 0.01 * nrm(ks[10], (DEPTH, MLSTM_HEADS)),
        'head_norm_g': 1.0 + 0.01 * nrm(ks[11], (DEPTH, MLSTM_HEADS, MLSTM_HEAD_DIM)),
        'w_mix_out': nrm(ks[12], (DEPTH, MIX_DIM, D_MODEL)) * MIX_DIM ** -0.5,
        'norm2_g': 1.0 + 0.01 * nrm(ks[13], (DEPTH, D_MODEL)),
        'w_gate': nrm(ks[14], (DEPTH, D_MODEL, D_FF)) * D_MODEL ** -0.5,
        'w_up': nrm(ks[15], (DEPTH, D_MODEL, D_FF)) * D_MODEL ** -0.5,
        'w_down': nrm(ks[16], (DEPTH, D_FF, D_MODEL)) * D_FF ** -0.5,
        'final_norm_g': 1.0 + 0.01 * nrm(ks[17], (D_MODEL,)),
    }


def reference(x_prompt, x_sample, state_conv, state_mlstm_C, state_mlstm_n, state_mlstm_m,
              norm1_g, w_mix_in, conv_w, b_igate, b_fgate, head_norm_g, w_mix_out, norm2_g,
              w_gate, w_up, w_down, final_norm_g):
    f32 = jnp.float32
    Bp = x_prompt.shape[0]
    Bs, Ts = x_sample.shape[0], x_sample.shape[1]
    hp, hs = x_prompt, x_sample
    conv_p, C_p, n_p, m_p = [], [], [], []
    conv_s, C_s, n_s, m_s = [], [], [], []
    for l in range(DEPTH):
        w = (norm1_g[l], w_mix_in[l], conv_w[l], b_igate[l], b_fgate[l], head_norm_g[l],
             w_mix_out[l], norm2_g[l], w_gate[l], w_up[l], w_down[l])
        cz = jnp.zeros((Bp, CONV_WIDTH - 1, CONV_DIM), dtype=hp.dtype)
        Cz = jnp.zeros((Bp, MLSTM_HEADS, MLSTM_HEAD_DIM, MLSTM_HEAD_DIM), dtype=f32)
        nz = jnp.zeros((Bp, MLSTM_HEADS, MLSTM_HEAD_DIM), dtype=f32)
        mz = jnp.zeros((Bp, MLSTM_HEADS), dtype=f32)
        hp, c1, C1, n1, m1 = block(hp, cz, Cz, nz, mz, CHUNK, *w)
        conv_p.append(c1); C_p.append(C1); n_p.append(n1); m_p.append(m1)
        hs, c2, C2, n2, m2 = block(hs, state_conv[l], state_mlstm_C[l].astype(f32),
                                   state_mlstm_n[l].astype(f32), state_mlstm_m[l].astype(f32), Ts, *w)
        conv_s.append(c2); C_s.append(C2); n_s.append(n2); m_s.append(m2)
    y_prompt = rms_norm(hp, final_norm_g)
    y_sample = rms_norm(hs, final_norm_g)
    return (y_prompt, y_sample,
            jnp.stack(conv_p), jnp.stack(C_p), jnp.stack(n_p), jnp.stack(m_p),
            jnp.stack(conv_s), jnp.stack(C_s), jnp.stack(n_s), jnp.stack(m_s))
```

```python
import functools

import jax
import jax.numpy as jnp
from jax import lax
from jax.experimental import pallas as pl
from jax.experimental.pallas import tpu as pltpu

D_MODEL = 1024
CONV_DIM = 512
CONV_WIDTH = 3
HEADS = 4
HEAD_DIM = 128
MLSTM_DIM = HEADS * HEAD_DIM
MIX_DIM = CONV_DIM + MLSTM_DIM
D_FF = 2816
EPS = 1e-6

LANES = 128
GATE_PAD = LANES
COL_CONV = 0
COL_QKV = 3 * CONV_DIM
COL_O = COL_QKV + 3 * MLSTM_DIM
PROJ_PAD = COL_O + MLSTM_DIM + GATE_PAD
VMEM_LIMIT = 56 * 1024 * 1024

BF16 = jnp.bfloat16
F32 = jnp.float32


def _rms_norm(x, g):
    ms = jnp.mean(x * x, axis=-1, keepdims=True)
    return x * lax.rsqrt(ms + EPS) * g


def _sigmoid(x):
    return 1.0 / (1.0 + jnp.exp(-x))


def _log_sigmoid(x):
    return jnp.minimum(x, 0.0) - jnp.log1p(jnp.exp(-jnp.abs(x)))


def _dot(a, b):
    return jnp.dot(a, b, preferred_element_type=F32)


def _mixer_kernel(x_ref, g1_ref, win_ref, convw_ref, gbias_ref, hng_ref, wout_ref,
                  conv0_ref, c0_ref, n0_ref, m0_ref,
                  xo_ref, conv_ref, c_ref, n_ref, m_ref,
                  hb_ref, pc_ref, z_ref, q_ref, k_ref, v_ref, o_ref, mix_ref,
                  *, tt, chunk, t_valid, row_chunk):
    t_idx = pl.program_id(1)

    @pl.when(t_idx == 0)
    def _():
        conv_ref[...] = conv0_ref[...]
        c_ref[...] = c0_ref[...]
        n_ref[...] = n0_ref[...]
        m_ref[...] = m0_ref[...]

    hb_ref[...] = _rms_norm(x_ref[0], g1_ref[...]).astype(BF16)

    pc_ref[...] = _dot(hb_ref[...], win_ref[:, COL_CONV:COL_QKV])
    z_ref[6:8, :] = conv_ref[0]
    for r0 in range(0, tt, row_chunk):
        rows = slice(r0, r0 + row_chunk)
        z_ref[8 + r0:8 + r0 + row_chunk, :] = (
            pc_ref[rows, 2 * CONV_DIM:3 * CONV_DIM] * pc_ref[rows, 0:CONV_DIM])
    w0 = convw_ref[0:1, :]
    w1 = convw_ref[1:2, :]
    w2 = convw_ref[2:3, :]
    for r0 in range(0, tt, row_chunk):
        rows = slice(r0, r0 + row_chunk)
        y = (z_ref[6 + r0:6 + r0 + row_chunk, :] * w0
             + z_ref[7 + r0:7 + r0 + row_chunk, :] * w1
             + z_ref[8 + r0:8 + r0 + row_chunk, :] * w2)
        mix_ref[rows, 0:CONV_DIM] = (pc_ref[rows, CONV_DIM:2 * CONV_DIM] * y).astype(BF16)
    conv_ref[0] = z_ref[8 + t_valid - 2:8 + t_valid, :]

    scale = HEAD_DIM ** -0.5
    q_ref[...] = _dot(hb_ref[...], win_ref[:, COL_QKV:COL_QKV + MLSTM_DIM]).astype(BF16)
    k_ref[...] = (_dot(hb_ref[...], win_ref[:, COL_QKV + MLSTM_DIM:COL_QKV + 2 * MLSTM_DIM])
                  * scale).astype(BF16)
    v_ref[...] = _dot(hb_ref[...], win_ref[:, COL_QKV + 2 * MLSTM_DIM:COL_O]).astype(BF16)
    o_ref[...] = _sigmoid(_dot(hb_ref[...], win_ref[:, COL_O:COL_O + MLSTM_DIM]))
    gates = _dot(hb_ref[...], win_ref[:, COL_O + MLSTM_DIM:PROJ_PAD]) + gbias_ref[...]

    lane = lax.broadcasted_iota(jnp.int32, (chunk, LANES), 1)
    is_igate = lane < HEADS
    row_i = lax.broadcasted_iota(jnp.int32, (chunk, chunk), 0)
    col_i = lax.broadcasted_iota(jnp.int32, (chunk, chunk), 1)
    causal = row_i >= col_i
    tri = causal.astype(F32)
    lane_row = lax.broadcasted_iota(jnp.int32, (1, LANES), 1)

    n_chunks = tt // chunk
    for c in range(n_chunks):
        rows = slice(c * chunk, (c + 1) * chunk)
        lv = min(chunk, t_valid - c * chunk)
        g_c = gates[rows, :]
        p_c = jnp.where(is_igate, g_c, _log_sigmoid(g_c))
        b_c = jnp.dot(tri, p_c, preferred_element_type=F32, precision=lax.Precision.HIGHEST)
        gq = jnp.where(is_igate, p_c, b_c)
        gqt = gq.T
        m_all = m_ref[0]
        m_new_all = m_all
        for h in range(HEADS):
            cols = slice(h * HEAD_DIM, (h + 1) * HEAD_DIM)
            li_col = gq[:, h:h + 1]
            b_col = gq[:, HEADS + h:HEADS + h + 1]
            li_row = gqt[h:h + 1, :]
            b_row = gqt[HEADS + h:HEADS + h + 1, :]
            m0 = m_all[:, HEADS + h:HEADS + h + 1]

            log_w = jnp.where(causal, b_col + (li_row - b_row), -jnp.inf)
            log_g = b_col + m0
            m_row = jnp.maximum(log_g, jnp.max(log_w, axis=-1, keepdims=True))
            w_intra = jnp.exp(log_w - m_row)
            w_carry = jnp.exp(log_g - m_row)

            q = q_ref[rows, cols]
            k = k_ref[rows, cols]
            v = v_ref[rows, cols]
            c_old = c_ref[0, h]
            n_old = n_ref[0, h:h + 1, :]
            s = lax.dot_general(q, k, (((1,), (1,)), ((), ())),
                                preferred_element_type=F32) * w_intra
            num = w_carry * _dot(q, c_old.astype(BF16)) + _dot(s.astype(BF16), v)
            qn = jnp.sum(q.astype(F32) * n_old, axis=-1, keepdims=True)
            den = w_carry * qn + jnp.sum(s, axis=-1, keepdims=True)
            h_cell = num / jnp.maximum(jnp.abs(den), jnp.exp(-m_row))

            mu = jnp.mean(h_cell, axis=-1, keepdims=True)
            d = h_cell - mu
            var = jnp.mean(d * d, axis=-1, keepdims=True)
            hn = d * lax.rsqrt(var + EPS) * hng_ref[:, cols]
            mix_ref[rows, CONV_DIM + h * HEAD_DIM:CONV_DIM + (h + 1) * HEAD_DIM] = (
                hn * o_ref[rows, cols]).astype(BF16)

            m_new = m_row[lv - 1:lv, :]
            b_last = b_col[lv - 1:lv, :]
            decay = jnp.exp(b_last + m0 - m_new)
            w_s = jnp.exp(b_last - b_col + li_col - m_new)
            if lv < chunk:
                w_s = jnp.where(row_i[:, 0:1] < lv, w_s, 0.0)
            kw = k.astype(F32) * w_s
            c_ref[0, h] = decay * c_old + _dot(kw.T.astype(BF16), v)
            n_ref[0, h:h + 1, :] = decay * n_old + jnp.sum(kw, axis=0, keepdims=True)
            m_new_all = jnp.where(lane_row == HEADS + h, m_new, m_new_all)
        m_ref[0] = m_new_all

    xo_ref[0] = x_ref[0] + _dot(mix_ref[...], wout_ref[...])


def _const_spec(shape):
    n = len(shape)
    return pl.BlockSpec(shape, lambda b, t: (0,) * n, pipeline_mode=pl.Buffered(1))


def _mixer(x, norm_g, w_in, conv_w, gate_bias, head_norm_g, w_out, conv0, c0, n0, m0,
           *, tt, chunk, t_valid, name):
    bsz, t_len, _ = x.shape
    kern = functools.partial(_mixer_kernel, tt=tt, chunk=chunk, t_valid=t_valid,
                             row_chunk=min(64, tt))
    per_seq = lambda *tail: pl.BlockSpec((1,) + tail, lambda b, t: (b,) + (0,) * len(tail))
    out_shape = (
        jax.ShapeDtypeStruct((bsz, t_len, D_MODEL), F32),
        jax.ShapeDtypeStruct((bsz, CONV_WIDTH - 1, CONV_DIM), F32),
        jax.ShapeDtypeStruct((bsz, HEADS, HEAD_DIM, HEAD_DIM), F32),
        jax.ShapeDtypeStruct((bsz, HEADS, HEAD_DIM), F32),
        jax.ShapeDtypeStruct((bsz, 1, LANES), F32),
    )
    x_spec = pl.BlockSpec((1, tt, D_MODEL), lambda b, t: (b, t, 0))
    state_specs = [per_seq(CONV_WIDTH - 1, CONV_DIM), per_seq(HEADS, HEAD_DIM, HEAD_DIM),
                   per_seq(HEADS, HEAD_DIM), per_seq(1, LANES)]
    return pl.pallas_call(
        kern,
        grid=(bsz, t_len // tt),
        in_specs=[x_spec, _const_spec((1, D_MODEL)), _const_spec((D_MODEL, PROJ_PAD)),
                  _const_spec((CONV_WIDTH, CONV_DIM)), _const_spec((1, GATE_PAD)),
                  _const_spec((1, MLSTM_DIM)), _const_spec((MIX_DIM, D_MODEL))] + state_specs,
        out_specs=[x_spec] + state_specs,
        out_shape=out_shape,
        scratch_shapes=[
            pltpu.VMEM((tt, D_MODEL), BF16),
            pltpu.VMEM((tt, 3 * CONV_DIM), F32),
            pltpu.VMEM((tt + 8, CONV_DIM), F32),
            pltpu.VMEM((tt, MLSTM_DIM), BF16),
            pltpu.VMEM((tt, MLSTM_DIM), BF16),
            pltpu.VMEM((tt, MLSTM_DIM), BF16),
            pltpu.VMEM((tt, MLSTM_DIM), F32),
            pltpu.VMEM((tt, MIX_DIM), BF16),
        ],
        compiler_params=pltpu.CompilerParams(
            dimension_semantics=("arbitrary", "arbitrary"), vmem_limit_bytes=VMEM_LIMIT),
        name=name,
    )(x, norm_g, w_in, conv_w, gate_bias, head_norm_g, w_out, conv0, c0, n0, m0)


def _ffn_kernel(x_ref, g2_ref, wg_ref, wu_ref, wd_ref, gf_ref, o_ref, hb_ref, *, f_chunk,
                final_norm):
    x = x_ref[...]
    hb_ref[...] = _rms_norm(x, g2_ref[...]).astype(BF16)
    acc = x
    for f0 in range(0, D_FF, f_chunk):
        g = _dot(hb_ref[...], wg_ref[:, f0:f0 + f_chunk])
        u = _dot(hb_ref[...], wu_ref[:, f0:f0 + f_chunk])
        a = (g * _sigmoid(g) * u).astype(BF16)
        acc = acc + _dot(a, wd_ref[f0:f0 + f_chunk, :])
    if final_norm:
        acc = _rms_norm(acc, gf_ref[...])
    o_ref[...] = acc


def _ffn(x, norm_g, w_gate, w_up, w_down, final_g, *, tm, final_norm, name):
    rows = x.shape[0]
    kern = functools.partial(_ffn_kernel, f_chunk=D_FF // 2, final_norm=final_norm)
    const = lambda shape: pl.BlockSpec(shape, lambda i: (0, 0), pipeline_mode=pl.Buffered(1))
    x_spec = pl.BlockSpec((tm, D_MODEL), lambda i: (i, 0))
    return pl.pallas_call(
        kern,
        grid=(rows // tm,),
        in_specs=[x_spec, const((1, D_MODEL)), const((D_MODEL, D_FF)), const((D_MODEL, D_FF)),
                  const((D_FF, D_MODEL)), const((1, D_MODEL))],
        out_specs=x_spec,
        out_shape=jax.ShapeDtypeStruct((rows, D_MODEL), F32),
        scratch_shapes=[pltpu.VMEM((tm, D_MODEL), BF16)],
        compiler_params=pltpu.CompilerParams(
            dimension_semantics=("arbitrary",), vmem_limit_bytes=VMEM_LIMIT),
        name=name,
    )(x, norm_g, w_gate, w_up, w_down, final_g)


PROMPT_TILE = 512
PROMPT_CHUNK = 256
SAMPLE_PAD = 128
FFN_TILE = 512


def kernel(x_prompt, x_sample, state_conv, state_mlstm_C, state_mlstm_n, state_mlstm_m, norm1_g, w_mix_in, conv_w, b_igate, b_fgate, head_norm_g, w_mix_out, norm2_g, w_gate, w_up, w_down, final_norm_g):
    depth = w_mix_in.shape[0]
    bp, tp, _ = x_prompt.shape
    bs, ts, _ = x_sample.shape

    hp = x_prompt
    hs = x_sample
    final_g = final_norm_g.reshape(1, D_MODEL)

    zeros_p = (jnp.zeros((bp, CONV_WIDTH - 1, CONV_DIM), F32),
               jnp.zeros((bp, HEADS, HEAD_DIM, HEAD_DIM), F32),
               jnp.zeros((bp, HEADS, HEAD_DIM), F32),
               jnp.zeros((bp, 1, LANES), F32))

    outs_p, outs_s = [], []
    for l in range(depth):
        w_in = jnp.pad(w_mix_in[l], ((0, 0), (0, PROJ_PAD - w_mix_in.shape[2]))).astype(BF16)
        w_out = w_mix_out[l].astype(BF16)
        wg, wu, wd = w_gate[l].astype(BF16), w_up[l].astype(BF16), w_down[l].astype(BF16)
        gate_bias = jnp.zeros((1, GATE_PAD), F32)
        gate_bias = gate_bias.at[0, 0:HEADS].set(b_igate[l]).at[0, HEADS:2 * HEADS].set(b_fgate[l])
        mixer_w = (norm1_g[l].reshape(1, D_MODEL), w_in, conv_w[l], gate_bias,
                   head_norm_g[l].reshape(1, MLSTM_DIM), w_out)
        ffn_w = (norm2_g[l].reshape(1, D_MODEL), wg, wu, wd, final_g)
        last = l == depth - 1

        hp, conv_p, c_p, n_p, m_p = _mixer(
            hp, *mixer_w, *zeros_p, tt=PROMPT_TILE, chunk=PROMPT_CHUNK, t_valid=PROMPT_TILE,
            name=f"mixer_prompt_{l}")
        hp = _ffn(hp.reshape(bp * tp, D_MODEL), *ffn_w, tm=FFN_TILE, final_norm=last,
                  name=f"ffn_prompt_{l}").reshape(bp, tp, D_MODEL)
        outs_p.append((conv_p, c_p, n_p, m_p[:, 0, HEADS:2 * HEADS]))

        m0 = jnp.zeros((bs, 1, LANES), F32).at[:, 0, HEADS:2 * HEADS].set(state_mlstm_m[l])
        hs_pad = jnp.pad(hs, ((0, 0), (0, SAMPLE_PAD - ts), (0, 0)))
        hs_pad, conv_s, c_s, n_s, m_s = _mixer(
            hs_pad, *mixer_w, state_conv[l], state_mlstm_C[l], state_mlstm_n[l], m0,
            tt=SAMPLE_PAD, chunk=SAMPLE_PAD, t_valid=ts, name=f"mixer_sample_{l}")
        hs = _ffn(hs_pad[:, :ts].reshape(bs * ts, D_MODEL), *ffn_w, tm=bs * ts, final_norm=last,
                  name=f"ffn_sample_{l}").reshape(bs, ts, D_MODEL)
        outs_s.append((conv_s, c_s, n_s, m_s[:, 0, HEADS:2 * HEADS]))

    stack = lambda outs, i: jnp.stack([o[i] for o in outs])
    return (hp, hs,
            stack(outs_p, 0), stack(outs_p, 1), stack(outs_p, 2), stack(outs_p, 3),
            stack(outs_s, 0), stack(outs_s, 1), stack(outs_s, 2), stack(outs_s, 3))
```

```python
import functools

import jax
import jax.numpy as jnp
from jax import lax
from jax.experimental import pallas as pl
from jax.experimental.pallas import tpu as pltpu

D_MODEL = 1024
CONV_DIM = 512
CONV_WIDTH = 3
HEADS = 4
HEAD_DIM = 128
MLSTM_DIM = HEADS * HEAD_DIM
MIX_DIM = CONV_DIM + MLSTM_DIM
D_FF = 2816
EPS = 1e-6

LANES = 128
BF16_ROWS = 16
STATE_ROWS = HEAD_DIM + BF16_ROWS
VMEM_LIMIT = 56 * 1024 * 1024

BF16 = jnp.bfloat16
F32 = jnp.float32
_NT = (((1,), (1,)), ((), ()))


def _rms_norm(x, g):
    ms = jnp.mean(x * x, axis=-1, keepdims=True)
    return x * lax.rsqrt(ms + EPS) * g


def _sigmoid(x):
    return 1.0 / (1.0 + jnp.exp(-x))


def _log_sigmoid(x):
    return jnp.minimum(x, 0.0) - jnp.log1p(jnp.exp(-jnp.abs(x)))


def _dot(a, b):
    return jnp.dot(a, b, preferred_element_type=F32)


def _mixer_kernel(x_ref, g1_ref, wnk_ref, wt_ref, wg_ref, gbias_ref, convw_ref, hng_ref, ut_ref,
                  wout_ref, conv0_ref, cn0_ref, m0_ref,
                  xo_ref, conv_ref, cn_ref, m_ref,
                  hb_ref, pc_ref, z_ref, k_ref, qt_ref, vt_ref, ot_ref, mix_ref,
                  *, tt, chunk, t_valid, row_chunk):
    t_idx = pl.program_id(1)

    @pl.when(t_idx == 0)
    def _():
        conv_ref[...] = conv0_ref[...]
        cn_ref[...] = cn0_ref[...]
        m_ref[...] = m0_ref[...]

    hb_ref[...] = _rms_norm(x_ref[0], g1_ref[...]).astype(BF16)
    n_lane_tiles = tt // LANES

    graw = lax.dot_general(wg_ref[...], hb_ref[...], _NT, preferred_element_type=F32)
    graw = graw + jnp.concatenate([gbias_ref[...]] * n_lane_tiles, axis=1)
    li = graw[0:8]
    lf = _log_sigmoid(graw[8:16])
    p1 = lf.astype(BF16).astype(F32)
    p2 = (lf - p1).astype(BF16).astype(F32)
    p3 = lf - p1 - p2
    pieces = jnp.concatenate([p1, p2, p3, jnp.zeros_like(p1)], axis=0).astype(BF16)
    csum = _dot(pieces, ut_ref[...])
    bt = csum[0:8] + csum[8:16] + csum[16:24]
    a = li - bt
    lane_t = lax.broadcasted_iota(jnp.int32, (8, tt), 1)
    cmax = a
    shift = 1
    while shift < tt:
        cmax = jnp.maximum(cmax, jnp.where(lane_t >= shift, pltpu.roll(cmax, shift, axis=1),
                                           -jnp.inf))
        shift *= 2
    m_start = m_ref[0]
    mt = bt + jnp.maximum(jnp.concatenate([m_start] * n_lane_tiles, axis=1), cmax)
    u = bt - mt
    eneg = jnp.exp(-mt)
    a_cols = jnp.concatenate([a, jnp.zeros((LANES - 8, tt), F32)], axis=0).T
    m_ref[0] = jnp.broadcast_to(mt[:, t_valid - 1:t_valid], (8, LANES))

    pc_ref[...] = _dot(hb_ref[...], wnk_ref[:, 0:3 * CONV_DIM])
    z_ref[6:8, :] = conv_ref[0]
    for r0 in range(0, tt, row_chunk):
        rows = slice(r0, r0 + row_chunk)
        z_ref[8 + r0:8 + r0 + row_chunk, :] = (
            pc_ref[rows, 2 * CONV_DIM:3 * CONV_DIM] * pc_ref[rows, 0:CONV_DIM])
    w0 = convw_ref[0:1, :]
    w1 = convw_ref[1:2, :]
    w2 = convw_ref[2:3, :]
    for r0 in range(0, tt, row_chunk):
        rows = slice(r0, r0 + row_chunk)
        y = (z_ref[6 + r0:6 + r0 + row_chunk, :] * w0
             + z_ref[7 + r0:7 + r0 + row_chunk, :] * w1
             + z_ref[8 + r0:8 + r0 + row_chunk, :] * w2)
        mix_ref[rows, 0:CONV_DIM] = (pc_ref[rows, CONV_DIM:2 * CONV_DIM] * y).astype(BF16)
    conv_ref[0] = z_ref[8 + t_valid - 2:8 + t_valid, :]

    k_ref[...] = (_dot(hb_ref[...], wnk_ref[:, 3 * CONV_DIM:3 * CONV_DIM + MLSTM_DIM])
                  * HEAD_DIM ** -0.5).astype(BF16)
    qt_ref[...] = lax.dot_general(wt_ref[0:MLSTM_DIM, :], hb_ref[...], _NT,
                                  preferred_element_type=F32).astype(BF16)
    vt = lax.dot_general(wt_ref[MLSTM_DIM:2 * MLSTM_DIM, :], hb_ref[...], _NT,
                         preferred_element_type=F32)
    for h in range(HEADS):
        vt_ref[h, 0:HEAD_DIM, :] = vt[h * HEAD_DIM:(h + 1) * HEAD_DIM, :].astype(BF16)
        vt_ref[h, HEAD_DIM:STATE_ROWS, :] = jnp.ones((STATE_ROWS - HEAD_DIM, tt), BF16)
    ot_ref[...] = _sigmoid(lax.dot_general(wt_ref[2 * MLSTM_DIM:3 * MLSTM_DIM, :], hb_ref[...], _NT,
                                           preferred_element_type=F32))

    row_i = lax.broadcasted_iota(jnp.int32, (chunk, chunk), 0)
    col_i = lax.broadcasted_iota(jnp.int32, (chunk, chunk), 1)
    causal_t = row_i <= col_i
    lane_c = lax.broadcasted_iota(jnp.int32, (8, chunk), 1)
    c_lane_tiles = chunk // LANES

    for c in range(tt // chunk):
        cs = c * chunk
        lv = min(chunk, t_valid - cs)
        if lv <= 0:
            break
        ce = cs + lv - 1
        rows = slice(cs, cs + chunk)
        if c == 0:
            kappa = jnp.concatenate([m_start] * c_lane_tiles, axis=1)
        else:
            kappa = jnp.broadcast_to(-u[:, cs - 1:cs], (8, chunk))
        u_end = u[:, ce:ce + 1]
        w_carry = jnp.exp(u[:, rows] + kappa)
        w_state = jnp.exp(a[:, rows] + jnp.broadcast_to(u_end, (8, chunk)))
        if lv < chunk:
            w_state = jnp.where(lane_c < lv, w_state, 0.0)
        decay = jnp.exp(jnp.broadcast_to(u_end, (8, LANES)) + kappa[:, 0:LANES])
        for h in range(HEADS):
            hcols = slice(h * HEAD_DIM, (h + 1) * HEAD_DIM)
            w_t = jnp.exp(jnp.where(causal_t, a_cols[rows, h:h + 1] + u[h:h + 1, rows], -jnp.inf))
            k = k_ref[rows, hcols]
            q_t = qt_ref[hcols, rows]
            v_t = vt_ref[h, :, rows]
            s_t = (_dot(k, q_t) * w_t).astype(BF16)
            cn = cn_ref[0, h]
            tot = w_carry[h:h + 1, :] * _dot(cn.astype(BF16), q_t) + _dot(v_t, s_t)
            den = tot[HEAD_DIM:HEAD_DIM + 1, :]
            inv = 1.0 / jnp.maximum(jnp.abs(den), eneg[h:h + 1, rows])
            h_t = tot[0:HEAD_DIM, :] * inv
            mu = jnp.mean(h_t, axis=0, keepdims=True)
            d = h_t - mu
            var = jnp.mean(d * d, axis=0, keepdims=True)
            g = jnp.concatenate([hng_ref[hcols, :]] * c_lane_tiles, axis=1)
            y_t = d * lax.rsqrt(var + EPS) * g * ot_ref[hcols, rows]
            mix_ref[rows, CONV_DIM + h * HEAD_DIM:CONV_DIM + (h + 1) * HEAD_DIM] = (
                y_t.T.astype(BF16))
            v_w = (v_t.astype(F32) * w_state[h:h + 1, :]).astype(BF16)
            cn_ref[0, h] = decay[h:h + 1, :] * cn + _dot(v_w, k)

    xo_ref[0] = x_ref[0] + _dot(mix_ref[...], wout_ref[...])


def _const_spec(shape):
    n = len(shape)
    return pl.BlockSpec(shape, lambda b, t: (0,) * n, pipeline_mode=pl.Buffered(1))


def _mixer(x, norm_g, w_nk, w_t, w_g, gate_bias, conv_w, head_norm_g, w_out, conv0, cn0, m0,
           *, tt, chunk, t_valid, name):
    bsz, t_len, _ = x.shape
    kern = functools.partial(_mixer_kernel, tt=tt, chunk=chunk, t_valid=t_valid,
                             row_chunk=min(64, tt))
    per_seq = lambda *tail: pl.BlockSpec((1,) + tail, lambda b, t: (b,) + (0,) * len(tail))
    out_shape = (
        jax.ShapeDtypeStruct((bsz, t_len, D_MODEL), F32),
        jax.ShapeDtypeStruct((bsz, CONV_WIDTH - 1, CONV_DIM), F32),
        jax.ShapeDtypeStruct((bsz, HEADS, STATE_ROWS, HEAD_DIM), F32),
        jax.ShapeDtypeStruct((bsz, 8, LANES), F32),
    )
    x_spec = pl.BlockSpec((1, tt, D_MODEL), lambda b, t: (b, t, 0))
    state_specs = [per_seq(CONV_WIDTH - 1, CONV_DIM), per_seq(HEADS, STATE_ROWS, HEAD_DIM),
                   per_seq(8, LANES)]
    upper = (lax.broadcasted_iota(jnp.int32, (tt, tt), 0)
             <= lax.broadcasted_iota(jnp.int32, (tt, tt), 1)).astype(BF16)
    return pl.pallas_call(
        kern,
        grid=(bsz, t_len // tt),
        in_specs=[x_spec, _const_spec((1, D_MODEL)), _const_spec(w_nk.shape),
                  _const_spec(w_t.shape), _const_spec(w_g.shape), _const_spec(gate_bias.shape),
                  _const_spec((CONV_WIDTH, CONV_DIM)), _const_spec(head_norm_g.shape),
                  _const_spec((tt, tt)), _const_spec((MIX_DIM, D_MODEL))] + state_specs,
        out_specs=[x_spec] + state_specs,
        out_shape=out_shape,
        scratch_shapes=[
            pltpu.VMEM((tt, D_MODEL), BF16),
            pltpu.VMEM((tt, 3 * CONV_DIM), F32),
            pltpu.VMEM((tt + 8, CONV_DIM), F32),
            pltpu.VMEM((tt, MLSTM_DIM), BF16),
            pltpu.VMEM((MLSTM_DIM, tt), BF16),
            pltpu.VMEM((HEADS, STATE_ROWS, tt), BF16),
            pltpu.VMEM((MLSTM_DIM, tt), F32),
            pltpu.VMEM((tt, MIX_DIM), BF16),
        ],
        compiler_params=pltpu.CompilerParams(
            dimension_semantics=("arbitrary", "arbitrary"), vmem_limit_bytes=VMEM_LIMIT),
        name=name,
    )(x, norm_g, w_nk, w_t, w_g, gate_bias, conv_w, head_norm_g, upper, w_out, conv0, cn0, m0)


def _ffn_kernel(x_ref, g2_ref, wg_ref, wu_ref, wd_ref, gf_ref, o_ref, hb_ref, *, f_chunk,
                final_norm):
    x = x_ref[...]
    hb_ref[...] = _rms_norm(x, g2_ref[...]).astype(BF16)
    acc = x
    for f0 in range(0, D_FF, f_chunk):
        g = _dot(hb_ref[...], wg_ref[:, f0:f0 + f_chunk])
        u = _dot(hb_ref[...], wu_ref[:, f0:f0 + f_chunk])
        a = (g * _sigmoid(g) * u).astype(BF16)
        acc = acc + _dot(a, wd_ref[f0:f0 + f_chunk, :])
    if final_norm:
        acc = _rms_norm(acc, gf_ref[...])
    o_ref[...] = acc


def _ffn(x, norm_g, w_gate, w_up, w_down, final_g, *, tm, final_norm, name):
    rows = x.shape[0]
    kern = functools.partial(_ffn_kernel, f_chunk=D_FF // 2, final_norm=final_norm)
    const = lambda shape: pl.BlockSpec(shape, lambda i: (0, 0), pipeline_mode=pl.Buffered(1))
    x_spec = pl.BlockSpec((tm, D_MODEL), lambda i: (i, 0))
    return pl.pallas_call(
        kern,
        grid=(rows // tm,),
        in_specs=[x_spec, const((1, D_MODEL)), const((D_MODEL, D_FF)), const((D_MODEL, D_FF)),
                  const((D_FF, D_MODEL)), const((1, D_MODEL))],
        out_specs=x_spec,
        out_shape=jax.ShapeDtypeStruct((rows, D_MODEL), F32),
        scratch_shapes=[pltpu.VMEM((tm, D_MODEL), BF16)],
        compiler_params=pltpu.CompilerParams(
            dimension_semantics=("arbitrary",), vmem_limit_bytes=VMEM_LIMIT),
        name=name,
    )(x, norm_g, w_gate, w_up, w_down, final_g)


PROMPT_TILE = 512
PROMPT_CHUNK = 256
SAMPLE_PAD = 128
FFN_TILE = 512


def _pack_state(c, n):
    n_rows = jnp.broadcast_to(n[:, :, None, :], n.shape[:2] + (STATE_ROWS - HEAD_DIM, HEAD_DIM))
    return jnp.concatenate([jnp.swapaxes(c, -1, -2), n_rows], axis=2)


def _pack_m(m):
    m8 = jnp.pad(m, ((0, 0), (0, 8 - HEADS)))
    return jnp.broadcast_to(m8[:, :, None], m8.shape + (LANES,))


def _unpack_state(cn, m):
    return (jnp.swapaxes(cn[:, :, 0:HEAD_DIM, :], -1, -2), cn[:, :, HEAD_DIM, :], m[:, 0:HEADS, 0])


def kernel(x_prompt, x_sample, state_conv, state_mlstm_C, state_mlstm_n, state_mlstm_m, norm1_g, w_mix_in, conv_w, b_igate, b_fgate, head_norm_g, w_mix_out, norm2_g, w_gate, w_up, w_down, final_norm_g):
    depth = w_mix_in.shape[0]
    bp, tp, _ = x_prompt.shape
    bs, ts, _ = x_sample.shape

    hp = x_prompt
    hs = x_sample
    final_g = final_norm_g.reshape(1, D_MODEL)

    zeros_p = (jnp.zeros((bp, CONV_WIDTH - 1, CONV_DIM), F32),
               jnp.zeros((bp, HEADS, STATE_ROWS, HEAD_DIM), F32),
               jnp.zeros((bp, 8, LANES), F32))

    c_q = 3 * CONV_DIM
    c_k, c_v, c_o, c_g = c_q + MLSTM_DIM, c_q + 2 * MLSTM_DIM, c_q + 3 * MLSTM_DIM, c_q + 4 * MLSTM_DIM

    outs_p, outs_s = [], []
    for l in range(depth):
        w = w_mix_in[l]
        w_nk = jnp.concatenate([w[:, 0:c_q], w[:, c_k:c_v]], axis=1).astype(BF16)
        w_t = jnp.concatenate([w[:, c_q:c_k], w[:, c_v:c_o], w[:, c_o:c_g]], axis=1).T.astype(BF16)
        w_g = jnp.zeros((2 * 8, D_MODEL), F32)
        w_g = w_g.at[0:HEADS].set(w[:, c_g:c_g + HEADS].T).at[8:8 + HEADS].set(w[:, c_g + HEADS:].T)
        w_g = w_g.astype(BF16)
        gate_bias = jnp.zeros((2 * 8, LANES), F32)
        gate_bias = gate_bias.at[0:HEADS].set(jnp.broadcast_to(b_igate[l][:, None], (HEADS, LANES)))
        gate_bias = gate_bias.at[8:8 + HEADS].set(
            jnp.broadcast_to(b_fgate[l][:, None], (HEADS, LANES)))
        hng = jnp.broadcast_to(head_norm_g[l].reshape(MLSTM_DIM, 1), (MLSTM_DIM, LANES))
        w_out = w_mix_out[l].astype(BF16)
        wg, wu, wd = w_gate[l].astype(BF16), w_up[l].astype(BF16), w_down[l].astype(BF16)
        mixer_w = (norm1_g[l].reshape(1, D_MODEL), w_nk, w_t, w_g, gate_bias, conv_w[l], hng, w_out)
        ffn_w = (norm2_g[l].reshape(1, D_MODEL), wg, wu, wd, final_g)
        last = l == depth - 1

        hp, conv_p, cn_p, m_p = _mixer(
            hp, *mixer_w, *zeros_p, tt=PROMPT_TILE, chunk=PROMPT_CHUNK, t_valid=PROMPT_TILE,
            name=f"mixer_prompt_{l}")
        hp = _ffn(hp.reshape(bp * tp, D_MODEL), *ffn_w, tm=FFN_TILE, final_norm=last,
                  name=f"ffn_prompt_{l}").reshape(bp, tp, D_MODEL)
        outs_p.append((conv_p,) + _unpack_state(cn_p, m_p))

        hs_pad = jnp.pad(hs, ((0, 0), (0, SAMPLE_PAD - ts), (0, 0)))
        hs_pad, conv_s, cn_s, m_s = _mixer(
            hs_pad, *mixer_w, state_conv[l], _pack_state(state_mlstm_C[l], state_mlstm_n[l]),
            _pack_m(state_mlstm_m[l]),
            tt=SAMPLE_PAD, chunk=SAMPLE_PAD, t_valid=ts, name=f"mixer_sample_{l}")
        hs = _ffn(hs_pad[:, :ts].reshape(bs * ts, D_MODEL), *ffn_w, tm=bs * ts, final_norm=last,
                  name=f"ffn_sample_{l}").reshape(bs, ts, D_MODEL)
        outs_s.append((conv_s,) + _unpack_state(cn_s, m_s))

    stack = lambda outs, i: jnp.stack([o[i] for o in outs])
    return (hp, hs,
            stack(outs_p, 0), stack(outs_p, 1), stack(outs_p, 2), stack(outs_p, 3),
            stack(outs_s, 0), stack(outs_s, 1), stack(outs_s, 2), stack(outs_s, 3))
```

```python
import functools

import jax
import jax.numpy as jnp
from jax import lax
from jax.experimental import pallas as pl
from jax.experimental.pallas import tpu as pltpu

D_MODEL = 1024
CONV_DIM = 512
CONV_WIDTH = 3
HEADS = 4
HEAD_DIM = 128
MLSTM_DIM = HEADS * HEAD_DIM
MIX_DIM = CONV_DIM + MLSTM_DIM
D_FF = 2816
EPS = 1e-6

LANES = 128
BF16_ROWS = 16
STATE_ROWS = HEAD_DIM + BF16_ROWS
VMEM_LIMIT = 56 * 1024 * 1024

BF16 = jnp.bfloat16
F32 = jnp.float32
_NT = (((1,), (1,)), ((), ()))
GATE_ROWS = 2 * 8
ROW_V = GATE_ROWS + MLSTM_DIM
ROW_O = ROW_V + MLSTM_DIM
MXU_WIDTH = 256


def _rms_norm(x, g):
    ms = jnp.mean(x * x, axis=-1, keepdims=True)
    return x * lax.rsqrt(ms + EPS) * g


def _sigmoid(x):
    return 1.0 / (1.0 + jnp.exp(-x))


def _log_sigmoid(x):
    return jnp.minimum(x, 0.0) - jnp.log1p(jnp.exp(-jnp.abs(x)))


def _dot(a, b):
    return jnp.dot(a, b, preferred_element_type=F32)


def _mixer_kernel(x_ref, g1_ref, wnk_ref, wt_ref, gbias_ref, convw_ref, hng_ref, ut_ref,
                  wout_ref, conv0_ref, cn0_ref, m0_ref,
                  xo_ref, conv_ref, cn_ref, m_ref,
                  hb_ref, pc_ref, z_ref, k_ref, qt_ref, vt_ref, ot_ref, mix_ref,
                  *, tt, chunk, t_valid, row_chunk):
    t_idx = pl.program_id(1)

    @pl.when(t_idx == 0)
    def _():
        conv_ref[...] = conv0_ref[...]
        cn_ref[...] = cn0_ref[...]
        m_ref[...] = m0_ref[...]

    hb_ref[...] = _rms_norm(x_ref[0], g1_ref[...]).astype(BF16)
    n_lane_tiles = tt // LANES

    gq = lax.dot_general(wt_ref[0:ROW_V, :], hb_ref[...], _NT, preferred_element_type=F32)
    qt_ref[...] = gq[GATE_ROWS:, :].astype(BF16)
    graw = gq[0:GATE_ROWS, :] + jnp.concatenate([gbias_ref[...]] * n_lane_tiles, axis=1)
    li = graw[0:8]
    lf = _log_sigmoid(graw[8:16])
    p1 = lf.astype(BF16).astype(F32)
    p2 = (lf - p1).astype(BF16).astype(F32)
    p3 = lf - p1 - p2
    pieces = jnp.concatenate([p1, p2, p3, jnp.zeros_like(p1)], axis=0).astype(BF16)
    csum = _dot(pieces, ut_ref[...])
    bt = csum[0:8] + csum[8:16] + csum[16:24]
    a = li - bt
    lane_t = lax.broadcasted_iota(jnp.int32, (8, tt), 1)
    cmax = a
    shift = 1
    while shift < tt:
        cmax = jnp.maximum(cmax, jnp.where(lane_t >= shift, pltpu.roll(cmax, shift, axis=1),
                                           -jnp.inf))
        shift *= 2
    m_start = m_ref[0]
    mt = bt + jnp.maximum(jnp.concatenate([m_start] * n_lane_tiles, axis=1), cmax)
    u = bt - mt
    eneg = jnp.exp(-mt)
    a_cols = jnp.concatenate([a, jnp.zeros((LANES - 8, tt), F32)], axis=0).T
    m_ref[0] = jnp.broadcast_to(mt[:, t_valid - 1:t_valid], (8, LANES))

    pc_ref[...] = _dot(hb_ref[...], wnk_ref[:, 0:3 * CONV_DIM])
    z_ref[6:8, :] = conv_ref[0]
    for r0 in range(0, tt, row_chunk):
        rows = slice(r0, r0 + row_chunk)
        z_ref[8 + r0:8 + r0 + row_chunk, :] = (
            pc_ref[rows, 2 * CONV_DIM:3 * CONV_DIM] * pc_ref[rows, 0:CONV_DIM])
    w0 = convw_ref[0:1, :]
    w1 = convw_ref[1:2, :]
    w2 = convw_ref[2:3, :]
    for r0 in range(0, tt, row_chunk):
        rows = slice(r0, r0 + row_chunk)
        y = (z_ref[6 + r0:6 + r0 + row_chunk, :] * w0
             + z_ref[7 + r0:7 + r0 + row_chunk, :] * w1
             + z_ref[8 + r0:8 + r0 + row_chunk, :] * w2)
        mix_ref[rows, 0:CONV_DIM] = (pc_ref[rows, CONV_DIM:2 * CONV_DIM] * y).astype(BF16)
    conv_ref[0] = z_ref[8 + t_valid - 2:8 + t_valid, :]

    k_ref[...] = (_dot(hb_ref[...], wnk_ref[:, 3 * CONV_DIM:3 * CONV_DIM + MLSTM_DIM])
                  * HEAD_DIM ** -0.5).astype(BF16)
    vt = lax.dot_general(wt_ref[ROW_V:ROW_O, :], hb_ref[...], _NT, preferred_element_type=F32)
    for h in range(HEADS):
        vt_ref[h, 0:HEAD_DIM, :] = vt[h * HEAD_DIM:(h + 1) * HEAD_DIM, :].astype(BF16)
        vt_ref[h, HEAD_DIM:STATE_ROWS, :] = jnp.ones((STATE_ROWS - HEAD_DIM, tt), BF16)
    ot_ref[...] = _sigmoid(lax.dot_general(wt_ref[ROW_O:ROW_O + MLSTM_DIM, :], hb_ref[...], _NT,
                                           preferred_element_type=F32))

    row_i = lax.broadcasted_iota(jnp.int32, (chunk, chunk), 0)
    col_i = lax.broadcasted_iota(jnp.int32, (chunk, chunk), 1)
    causal_t = row_i <= col_i
    lane_c = lax.broadcasted_iota(jnp.int32, (8, chunk), 1)
    c_lane_tiles = chunk // LANES

    for c in range(tt // chunk):
        cs = c * chunk
        lv = min(chunk, t_valid - cs)
        if lv <= 0:
            break
        ce = cs + lv - 1
        rows = slice(cs, cs + chunk)
        if c == 0:
            kappa = jnp.concatenate([m_start] * c_lane_tiles, axis=1)
        else:
            kappa = jnp.broadcast_to(-u[:, cs - 1:cs], (8, chunk))
        u_end = u[:, ce:ce + 1]
        w_carry = jnp.exp(u[:, rows] + kappa)
        w_state = jnp.exp(a[:, rows] + jnp.broadcast_to(u_end, (8, chunk)))
        if lv < chunk:
            w_state = jnp.where(lane_c < lv, w_state, 0.0)
        decay = jnp.exp(jnp.broadcast_to(u_end, (8, LANES)) + kappa[:, 0:LANES])
        for h in range(HEADS):
            hcols = slice(h * HEAD_DIM, (h + 1) * HEAD_DIM)
            w_t = jnp.exp(jnp.where(causal_t, a_cols[rows, h:h + 1] + u[h:h + 1, rows], -jnp.inf))
            k = k_ref[rows, hcols]
            q_t = qt_ref[hcols, rows]
            v_t = vt_ref[h, :, rows]
            s_t = (_dot(k, q_t) * w_t).astype(BF16)
            cn = cn_ref[0, h]
            tot = w_carry[h:h + 1, :] * _dot(cn.astype(BF16), q_t) + _dot(v_t, s_t)
            den = tot[HEAD_DIM:HEAD_DIM + 1, :]
            inv = 1.0 / jnp.maximum(jnp.abs(den), eneg[h:h + 1, rows])
            h_t = tot[0:HEAD_DIM, :] * inv
            mu = jnp.mean(h_t, axis=0, keepdims=True)
            d = h_t - mu
            var = jnp.mean(d * d, axis=0, keepdims=True)
            g = jnp.concatenate([hng_ref[hcols, :]] * c_lane_tiles, axis=1)
            y_t = d * lax.rsqrt(var + EPS) * g * ot_ref[hcols, rows]
            mix_ref[rows, CONV_DIM + h * HEAD_DIM:CONV_DIM + (h + 1) * HEAD_DIM] = (
                y_t.T.astype(BF16))
            v_w = (v_t.astype(F32) * w_state[h:h + 1, :]).astype(BF16)
            cn_ref[0, h] = decay[h:h + 1, :] * cn + _dot(v_w, k)

    xo_ref[0] = x_ref[0] + _dot(mix_ref[...], wout_ref[...])


def _layer_spec(stacked, layer):
    tail = stacked.shape[1:]
    return pl.BlockSpec((None,) + tail, lambda *_: (layer,) + (0,) * len(tail),
                        pipeline_mode=pl.Buffered(1))


def _mixer(x, norm_g, w_nk, w_t, gate_bias, conv_w, head_norm_g, w_out, conv0, cn0, m0,
           *, layer, tt, chunk, t_valid, name):
    bsz, t_len, _ = x.shape
    layer_spec = lambda a: _layer_spec(a, layer)
    kern = functools.partial(_mixer_kernel, tt=tt, chunk=chunk, t_valid=t_valid,
                             row_chunk=min(64, tt))
    per_seq = lambda *tail: pl.BlockSpec((1,) + tail, lambda b, t: (b,) + (0,) * len(tail))
    out_shape = (
        jax.ShapeDtypeStruct((bsz, t_len, D_MODEL), F32),
        jax.ShapeDtypeStruct((bsz, CONV_WIDTH - 1, CONV_DIM), F32),
        jax.ShapeDtypeStruct((bsz, HEADS, STATE_ROWS, HEAD_DIM), F32),
        jax.ShapeDtypeStruct((bsz, 8, LANES), F32),
    )
    x_spec = pl.BlockSpec((1, tt, D_MODEL), lambda b, t: (b, t, 0))
    state_specs = [per_seq(CONV_WIDTH - 1, CONV_DIM), per_seq(HEADS, STATE_ROWS, HEAD_DIM),
                   per_seq(8, LANES)]
    upper = (lax.broadcasted_iota(jnp.int32, (tt, tt), 0)
             <= lax.broadcasted_iota(jnp.int32, (tt, tt), 1)).astype(BF16)
    return pl.pallas_call(
        kern,
        grid=(bsz, t_len // tt),
        in_specs=[x_spec, layer_spec(norm_g), layer_spec(w_nk), layer_spec(w_t),
                  layer_spec(gate_bias), layer_spec(conv_w), layer_spec(head_norm_g),
                  pl.BlockSpec((tt, tt), lambda b, t: (0, 0), pipeline_mode=pl.Buffered(1)),
                  layer_spec(w_out)] + state_specs,
        out_specs=[x_spec] + state_specs,
        out_shape=out_shape,
        scratch_shapes=[
            pltpu.VMEM((tt, D_MODEL), BF16),
            pltpu.VMEM((tt, 3 * CONV_DIM), F32),
            pltpu.VMEM((tt + 8, CONV_DIM), F32),
            pltpu.VMEM((tt, MLSTM_DIM), BF16),
            pltpu.VMEM((MLSTM_DIM, tt), BF16),
            pltpu.VMEM((HEADS, STATE_ROWS, tt), BF16),
            pltpu.VMEM((MLSTM_DIM, tt), F32),
            pltpu.VMEM((tt, MIX_DIM), BF16),
        ],
        compiler_params=pltpu.CompilerParams(
            dimension_semantics=("arbitrary", "arbitrary"), vmem_limit_bytes=VMEM_LIMIT),
        name=name,
    )(x, norm_g, w_nk, w_t, gate_bias, conv_w, head_norm_g, upper, w_out, conv0, cn0, m0)


def _ffn_kernel(x_ref, g2_ref, wg_ref, wu_ref, wd_ref, gf_ref, o_ref, hb_ref, *, f_split,
                final_norm):
    x = x_ref[...]
    hb_ref[...] = _rms_norm(x, g2_ref[...]).astype(BF16)
    acc = x
    for f0, f1 in ((0, f_split), (f_split, D_FF)):
        g = _dot(hb_ref[...], wg_ref[:, f0:f1])
        u = _dot(hb_ref[...], wu_ref[:, f0:f1])
        a = (g * _sigmoid(g) * u).astype(BF16)
        acc = acc + _dot(a, wd_ref[f0:f1, :])
    if final_norm:
        acc = _rms_norm(acc, gf_ref[...])
    o_ref[...] = acc


def _ffn(x, norm_g, w_gate, w_up, w_down, final_g, *, layer, tm, final_norm, name):
    rows = x.shape[0]
    f_split = (D_FF // MXU_WIDTH + 1) // 2 * MXU_WIDTH
    kern = functools.partial(_ffn_kernel, f_split=f_split, final_norm=final_norm)
    layer_spec = lambda a: _layer_spec(a, layer)
    x_spec = pl.BlockSpec((tm, D_MODEL), lambda i: (i, 0))
    return pl.pallas_call(
        kern,
        grid=(rows // tm,),
        in_specs=[x_spec, layer_spec(norm_g), layer_spec(w_gate), layer_spec(w_up),
                  layer_spec(w_down),
                  pl.BlockSpec((1, D_MODEL), lambda i: (0, 0), pipeline_mode=pl.Buffered(1))],
        out_specs=x_spec,
        out_shape=jax.ShapeDtypeStruct((rows, D_MODEL), F32),
        scratch_shapes=[pltpu.VMEM((tm, D_MODEL), BF16)],
        compiler_params=pltpu.CompilerParams(
            dimension_semantics=("arbitrary",), vmem_limit_bytes=VMEM_LIMIT),
        name=name,
    )(x, norm_g, w_gate, w_up, w_down, final_g)


PROMPT_TILE = 512
PROMPT_CHUNK = 256
SAMPLE_PAD = 128
FFN_TILE = 512


def _pack_state(c, n):
    n_rows = jnp.broadcast_to(n[..., None, :], n.shape[:-1] + (STATE_ROWS - HEAD_DIM, HEAD_DIM))
    return jnp.concatenate([jnp.swapaxes(c, -1, -2), n_rows], axis=-2)


def _pack_m(m):
    m8 = jnp.pad(m, ((0, 0),) * (m.ndim - 1) + ((0, 8 - HEADS),))
    return jnp.broadcast_to(m8[..., None], m8.shape + (LANES,))


def _unpack_state(cn, m):
    return (jnp.swapaxes(cn[..., 0:HEAD_DIM, :], -1, -2), cn[..., HEAD_DIM, :], m[..., 0:HEADS, 0])


def kernel(x_prompt, x_sample, state_conv, state_mlstm_C, state_mlstm_n, state_mlstm_m, norm1_g, w_mix_in, conv_w, b_igate, b_fgate, head_norm_g, w_mix_out, norm2_g, w_gate, w_up, w_down, final_norm_g):
    depth = w_mix_in.shape[0]
    bp, tp, _ = x_prompt.shape
    bs, ts, _ = x_sample.shape

    hp = x_prompt
    hs = x_sample
    final_g = final_norm_g.reshape(1, D_MODEL)

    zeros_p = (jnp.zeros((bp, CONV_WIDTH - 1, CONV_DIM), F32),
               jnp.zeros((bp, HEADS, STATE_ROWS, HEAD_DIM), F32),
               jnp.zeros((bp, 8, LANES), F32))

    c_q = 3 * CONV_DIM
    c_k, c_v, c_o, c_g = c_q + MLSTM_DIM, c_q + 2 * MLSTM_DIM, c_q + 3 * MLSTM_DIM, c_q + 4 * MLSTM_DIM

    w = w_mix_in
    w_nk = jnp.concatenate([w[:, :, 0:c_q], w[:, :, c_k:c_v]], axis=2).astype(BF16)
    gate_cols = jnp.zeros((depth, D_MODEL, GATE_ROWS), F32)
    gate_cols = gate_cols.at[:, :, 0:HEADS].set(w[:, :, c_g:c_g + HEADS])
    gate_cols = gate_cols.at[:, :, 8:8 + HEADS].set(w[:, :, c_g + HEADS:c_g + 2 * HEADS])
    w_t = jnp.swapaxes(jnp.concatenate(
        [gate_cols, w[:, :, c_q:c_k], w[:, :, c_v:c_o], w[:, :, c_o:c_g]], axis=2), 1, 2).astype(BF16)
    gate_bias = jnp.zeros((depth, GATE_ROWS), F32)
    gate_bias = gate_bias.at[:, 0:HEADS].set(b_igate).at[:, 8:8 + HEADS].set(b_fgate)
    gate_bias = jnp.broadcast_to(gate_bias[:, :, None], (depth, GATE_ROWS, LANES))
    hng = jnp.broadcast_to(head_norm_g.reshape(depth, MLSTM_DIM, 1), (depth, MLSTM_DIM, LANES))
    mixer_w = (norm1_g.reshape(depth, 1, D_MODEL), w_nk, w_t, gate_bias, conv_w, hng,
               w_mix_out.astype(BF16))
    ffn_w = (norm2_g.reshape(depth, 1, D_MODEL), w_gate.astype(BF16), w_up.astype(BF16),
             w_down.astype(BF16), final_g)
    cn_s0 = _pack_state(state_mlstm_C, state_mlstm_n)
    m_s0 = _pack_m(state_mlstm_m)

    outs_p, outs_s = [], []
    for l in range(depth):
        last = l == depth - 1
        hp, conv_p, cn_p, m_p = _mixer(
            hp, *mixer_w, *zeros_p, layer=l, tt=PROMPT_TILE, chunk=PROMPT_CHUNK,
            t_valid=PROMPT_TILE, name=f"mixer_prompt_{l}")
        hp = _ffn(hp.reshape(bp * tp, D_MODEL), *ffn_w, layer=l, tm=FFN_TILE, final_norm=last,
                  name=f"ffn_prompt_{l}").reshape(bp, tp, D_MODEL)
        outs_p.append((conv_p, cn_p, m_p))

        hs_pad = jnp.pad(hs, ((0, 0), (0, SAMPLE_PAD - ts), (0, 0)))
        hs_pad, conv_s, cn_s, m_s = _mixer(
            hs_pad, *mixer_w, state_conv[l], cn_s0[l], m_s0[l], layer=l,
            tt=SAMPLE_PAD, chunk=SAMPLE_PAD, t_valid=ts, name=f"mixer_sample_{l}")
        hs = _ffn(hs_pad[:, :ts].reshape(bs * ts, D_MODEL), *ffn_w, layer=l, tm=bs * ts,
                  final_norm=last, name=f"ffn_sample_{l}").reshape(bs, ts, D_MODEL)
        outs_s.append((conv_s, cn_s, m_s))

    def states(outs):
        conv = jnp.stack([o[0] for o in outs])
        c, n, m = _unpack_state(jnp.stack([o[1] for o in outs]), jnp.stack([o[2] for o in outs]))
        return conv, c, n, m

    return (hp, hs) + states(outs_p) + states(outs_s)
```

```python
import functools

import jax
import jax.numpy as jnp
from jax import lax
from jax.experimental import pallas as pl
from jax.experimental.pallas import tpu as pltpu

D_MODEL = 1024
CONV_DIM = 512
CONV_WIDTH = 3
HEADS = 4
HEAD_DIM = 128
MLSTM_DIM = HEADS * HEAD_DIM
MIX_DIM = CONV_DIM + MLSTM_DIM
D_FF = 2816
EPS = 1e-6

LANES = 128
BF16_ROWS = 16
MXU_WIDTH = 256
STATE_ROWS = HEAD_DIM + BF16_ROWS
GATE_ROWS = 2 * 8
ROW_V = GATE_ROWS + MLSTM_DIM
ROW_O = ROW_V + MLSTM_DIM
FFN_SPLIT = (D_FF // MXU_WIDTH + 1) // 2 * MXU_WIDTH
VMEM_LIMIT = 60 * 1024 * 1024

BF16 = jnp.bfloat16
F32 = jnp.float32
_NT = (((1,), (1,)), ((), ()))


def _rms_norm(x, g):
    ms = jnp.mean(x * x, axis=-1, keepdims=True)
    return x * lax.rsqrt(ms + EPS) * g


def _sigmoid(x):
    return 1.0 / (1.0 + jnp.exp(-x))


def _log_sigmoid(x):
    return jnp.minimum(x, 0.0) - jnp.log1p(jnp.exp(-jnp.abs(x)))


def _dot(a, b):
    return jnp.dot(a, b, preferred_element_type=F32)


def _mixer_tile(x_ref, w, state, scr, xo_ref, *, tt, chunk, t_valid, row_chunk):
    g1_ref, wnk_ref, wt_ref, gbias_ref, convw_ref, hng_ref, ut_ref, wout_ref = w
    conv_prev, cn_prev, m_start = state
    hb_ref, pc_ref, k_ref, qt_ref, vt_ref, ot_ref, mix_ref = scr

    hb_ref[...] = _rms_norm(x_ref[...], g1_ref[...]).astype(BF16)
    n_lane_tiles = tt // LANES

    gq = lax.dot_general(wt_ref[0:ROW_V, :], hb_ref[...], _NT, preferred_element_type=F32)
    qt_ref[...] = gq[GATE_ROWS:, :].astype(BF16)
    graw = gq[0:GATE_ROWS, :] + jnp.concatenate([gbias_ref[...]] * n_lane_tiles, axis=1)
    li = graw[0:8]
    lf = _log_sigmoid(graw[8:16])
    p1 = lf.astype(BF16).astype(F32)
    p2 = (lf - p1).astype(BF16).astype(F32)
    p3 = lf - p1 - p2
    pieces = jnp.concatenate([p1, p2, p3, jnp.zeros_like(p1)], axis=0).astype(BF16)
    csum = _dot(pieces, ut_ref[...])
    bt = csum[0:8] + csum[8:16] + csum[16:24]
    a = li - bt
    lane_t = lax.broadcasted_iota(jnp.int32, (8, tt), 1)
    cmax = a
    shift = 1
    while shift < tt:
        cmax = jnp.maximum(cmax, jnp.where(lane_t >= shift, pltpu.roll(cmax, shift, axis=1),
                                           -jnp.inf))
        shift *= 2
    mt = bt + jnp.maximum(jnp.concatenate([m_start] * n_lane_tiles, axis=1), cmax)
    u = bt - mt
    eneg = jnp.exp(-mt)
    a_cols = jnp.concatenate([a, jnp.zeros((LANES - 8, tt), F32)], axis=0).T
    m_new = jnp.broadcast_to(mt[:, t_valid - 1:t_valid], (8, LANES))

    pc_ref[...] = _dot(hb_ref[...], wnk_ref[:, 0:3 * CONV_DIM])
    w0 = convw_ref[0:1, :]
    w1 = convw_ref[1:2, :]
    w2 = convw_ref[2:3, :]
    row_c = lax.broadcasted_iota(jnp.int32, (row_chunk, CONV_DIM), 0)
    prev2, prev1 = conv_prev[0:1, :], conv_prev[1:2, :]
    for r0 in range(0, tt, row_chunk):
        rows = slice(r0, r0 + row_chunk)
        z = pc_ref[rows, 2 * CONV_DIM:3 * CONV_DIM] * pc_ref[rows, 0:CONV_DIM]
        z1 = jnp.where(row_c == 0, prev1, pltpu.roll(z, 1, axis=0))
        z2 = jnp.where(row_c == 0, prev2, jnp.where(row_c == 1, prev1, pltpu.roll(z, 2, axis=0)))
        y = z2 * w0 + z1 * w1 + z * w2
        mix_ref[rows, 0:CONV_DIM] = (pc_ref[rows, CONV_DIM:2 * CONV_DIM] * y).astype(BF16)
        prev2, prev1 = z[row_chunk - 2:row_chunk - 1, :], z[row_chunk - 1:row_chunk, :]
        if r0 <= t_valid - 2 < r0 + row_chunk:
            conv_new = z[t_valid - 2 - r0:t_valid - r0, :]

    k_ref[...] = (_dot(hb_ref[...], wnk_ref[:, 3 * CONV_DIM:3 * CONV_DIM + MLSTM_DIM])
                  * HEAD_DIM ** -0.5).astype(BF16)
    vt = lax.dot_general(wt_ref[ROW_V:ROW_O, :], hb_ref[...], _NT, preferred_element_type=F32)
    for h in range(HEADS):
        vt_ref[h, 0:HEAD_DIM, :] = vt[h * HEAD_DIM:(h + 1) * HEAD_DIM, :].astype(BF16)
        vt_ref[h, HEAD_DIM:STATE_ROWS, :] = jnp.ones((STATE_ROWS - HEAD_DIM, tt), BF16)
    ot_ref[...] = _sigmoid(lax.dot_general(wt_ref[ROW_O:ROW_O + MLSTM_DIM, :], hb_ref[...], _NT,
                                           preferred_element_type=F32))

    row_i = lax.broadcasted_iota(jnp.int32, (chunk, chunk), 0)
    col_i = lax.broadcasted_iota(jnp.int32, (chunk, chunk), 1)
    causal_t = row_i <= col_i
    lane_c = lax.broadcasted_iota(jnp.int32, (8, chunk), 1)
    c_lane_tiles = chunk // LANES

    cn_cur = list(cn_prev)
    for c in range(tt // chunk):
        cs = c * chunk
        lv = min(chunk, t_valid - cs)
        if lv <= 0:
            break
        ce = cs + lv - 1
        rows = slice(cs, cs + chunk)
        if c == 0:
            kappa = jnp.concatenate([m_start] * c_lane_tiles, axis=1)
        else:
            kappa = jnp.broadcast_to(-u[:, cs - 1:cs], (8, chunk))
        u_end = u[:, ce:ce + 1]
        w_carry = jnp.exp(u[:, rows] + kappa)
        w_state = jnp.exp(a[:, rows] + jnp.broadcast_to(u_end, (8, chunk)))
        if lv < chunk:
            w_state = jnp.where(lane_c < lv, w_state, 0.0)
        decay = jnp.exp(jnp.broadcast_to(u_end, (8, LANES)) + kappa[:, 0:LANES])
        for h in range(HEADS):
            hcols = slice(h * HEAD_DIM, (h + 1) * HEAD_DIM)
            w_t = jnp.exp(jnp.where(causal_t, a_cols[rows, h:h + 1] + u[h:h + 1, rows], -jnp.inf))
            k = k_ref[rows, hcols]
            q_t = qt_ref[hcols, rows]
            v_t = vt_ref[h, :, rows]
            s_t = (_dot(k, q_t) * w_t).astype(BF16)
            cn = cn_cur[h]
            tot = w_carry[h:h + 1, :] * _dot(cn.astype(BF16), q_t) + _dot(v_t, s_t)
            den = tot[HEAD_DIM:HEAD_DIM + 1, :]
            inv = 1.0 / jnp.maximum(jnp.abs(den), eneg[h:h + 1, rows])
            h_t = tot[0:HEAD_DIM, :] * inv
            mu = jnp.mean(h_t, axis=0, keepdims=True)
            d = h_t - mu
            var = jnp.mean(d * d, axis=0, keepdims=True)
            g = jnp.concatenate([hng_ref[hcols, :]] * c_lane_tiles, axis=1)
            y_t = d * lax.rsqrt(var + EPS) * g * ot_ref[hcols, rows]
            mix_ref[rows, CONV_DIM + h * HEAD_DIM:CONV_DIM + (h + 1) * HEAD_DIM] = (
                y_t.T.astype(BF16))
            v_w = (v_t.astype(F32) * w_state[h:h + 1, :]).astype(BF16)
            cn_cur[h] = decay[h:h + 1, :] * cn + _dot(v_w, k)

    xo = x_ref[...] + _dot(mix_ref[...], wout_ref[...])
    xo_ref[...] = xo
    return conv_new, cn_cur, m_new, xo


def _ffn_norm(x, g2_ref, hb_ref):
    hb_ref[...] = _rms_norm(x, g2_ref[...]).astype(BF16)


def _ffn_matmuls(x_ref, hb_ref, wg_ref, wu_ref, wd_ref, gf_ref, y_ref, *, final_norm):
    parts = ((0, FFN_SPLIT), (FFN_SPLIT, D_FF))
    for i, (f0, f1) in enumerate(parts):
        g = _dot(hb_ref[...], wg_ref[:, f0:f1])
        u = _dot(hb_ref[...], wu_ref[:, f0:f1])
        a = (g * _sigmoid(g) * u).astype(BF16)
        base = x_ref if i == 0 else y_ref
        acc = base[...] + _dot(a, wd_ref[f0:f1, :])
        if final_norm and i == len(parts) - 1:
            acc = _rms_norm(acc, gf_ref[...])
        y_ref[...] = acc


def _mixer_scratch(tt):
    return [
        pltpu.VMEM((tt, D_MODEL), BF16),
        pltpu.VMEM((tt, 3 * CONV_DIM), F32),
        pltpu.VMEM((tt, MLSTM_DIM), BF16),
        pltpu.VMEM((MLSTM_DIM, tt), BF16),
        pltpu.VMEM((HEADS, STATE_ROWS, tt), BF16),
        pltpu.VMEM((MLSTM_DIM, tt), F32),
        pltpu.VMEM((tt, MIX_DIM), BF16),
    ]


def _upper_ones(tt):
    return (lax.broadcasted_iota(jnp.int32, (tt, tt), 0)
            <= lax.broadcasted_iota(jnp.int32, (tt, tt), 1)).astype(BF16)


def _layer_spec(stacked, layer):
    tail = stacked.shape[1:]
    return pl.BlockSpec((None,) + tail, lambda *_: (layer,) + (0,) * len(tail),
                        pipeline_mode=pl.Buffered(1))


def _whole_spec(arr):
    return pl.BlockSpec(arr.shape, lambda *_: (0,) * arr.ndim, pipeline_mode=pl.Buffered(1))


def _layer_kernel(x_ref, g1_ref, wnk_ref, wt_ref, gbias_ref, convw_ref, hng_ref, ut_ref, wout_ref,
                  g2_ref, wg_ref, wu_ref, wd_ref, gf_ref, conv0_ref, cn0_ref, m0_ref,
                  y_ref, conv_ref, cn_ref, m_ref,
                  hb_ref, pc_ref, k_ref, qt_ref, vt_ref, ot_ref, mix_ref, x1_ref, hb2_ref,
                  *, tt, chunk, row_chunk, tiles_per_seq, n_tiles, final_norm):
    g = pl.program_id(0)

    @pl.when(g == 0)
    def _():
        x1_ref[...] = jnp.zeros_like(x1_ref)

    @pl.when(jnp.logical_and(lax.rem(g, tiles_per_seq) == 0, g < n_tiles))
    def _():
        conv_ref[...] = conv0_ref[...]
        cn_ref[...] = cn0_ref[...]
        m_ref[...] = m0_ref[...]

    x1 = x1_ref[...]
    _ffn_norm(x1, g2_ref, hb2_ref)
    y_ref[...] = x1

    state = (conv_ref[0], [cn_ref[0, h] for h in range(HEADS)], m_ref[0])
    conv_new, cn_new, m_new, _ = _mixer_tile(
        x_ref, (g1_ref, wnk_ref, wt_ref, gbias_ref, convw_ref, hng_ref, ut_ref, wout_ref), state,
        (hb_ref, pc_ref, k_ref, qt_ref, vt_ref, ot_ref, mix_ref), x1_ref,
        tt=tt, chunk=chunk, t_valid=tt, row_chunk=row_chunk)
    _ffn_matmuls(y_ref, hb2_ref, wg_ref, wu_ref, wd_ref, gf_ref, y_ref, final_norm=final_norm)
    live = g < n_tiles
    conv_ref[0] = jnp.where(live, conv_new, state[0])
    for h in range(HEADS):
        cn_ref[0, h] = jnp.where(live, cn_new[h], state[1][h])
    m_ref[0] = jnp.where(live, m_new, state[2])


def _prompt_layer(x, mixer_w, ffn_w, state0, *, layer, tt, chunk, final_norm, name):
    bsz, t_len, _ = x.shape
    tiles_per_seq = t_len // tt
    n_tiles = bsz * tiles_per_seq
    norm1_g, w_nk, w_t, gate_bias, conv_w, hng, w_out = mixer_w
    norm2_g, w_gate, w_up, w_down, final_g = ffn_w
    upper = _upper_ones(tt)
    kern = functools.partial(_layer_kernel, tt=tt, chunk=chunk, row_chunk=min(64, tt),
                             tiles_per_seq=tiles_per_seq, n_tiles=n_tiles, final_norm=final_norm)
    mixer_tile = lambda g: jnp.minimum(g, n_tiles - 1)
    seq_spec = lambda *tail: pl.BlockSpec(
        (1,) + tail, lambda g: (mixer_tile(g) // tiles_per_seq,) + (0,) * len(tail))
    state_specs = [seq_spec(CONV_WIDTH - 1, CONV_DIM), seq_spec(HEADS, STATE_ROWS, HEAD_DIM),
                   seq_spec(8, LANES)]
    ls = lambda arr: _layer_spec(arr, layer)
    y, conv, cn, m = pl.pallas_call(
        kern,
        grid=(n_tiles + 1,),
        in_specs=[pl.BlockSpec((tt, D_MODEL), lambda g: (mixer_tile(g), 0)),
                  ls(norm1_g), ls(w_nk), ls(w_t), ls(gate_bias), ls(conv_w), ls(hng),
                  _whole_spec(upper), ls(w_out),
                  ls(norm2_g), ls(w_gate), ls(w_up), ls(w_down), _whole_spec(final_g)]
        + state_specs,
        out_specs=[pl.BlockSpec((tt, D_MODEL), lambda g: (jnp.maximum(g - 1, 0), 0))] + state_specs,
        out_shape=(
            jax.ShapeDtypeStruct((bsz * t_len, D_MODEL), F32),
            jax.ShapeDtypeStruct((bsz, CONV_WIDTH - 1, CONV_DIM), F32),
            jax.ShapeDtypeStruct((bsz, HEADS, STATE_ROWS, HEAD_DIM), F32),
            jax.ShapeDtypeStruct((bsz, 8, LANES), F32),
        ),
        scratch_shapes=_mixer_scratch(tt) + [
            pltpu.VMEM((tt, D_MODEL), F32),
            pltpu.VMEM((tt, D_MODEL), BF16),
        ],
        compiler_params=pltpu.CompilerParams(
            dimension_semantics=("arbitrary",), vmem_limit_bytes=VMEM_LIMIT),
        name=name,
    )(x.reshape(bsz * t_len, D_MODEL), norm1_g, w_nk, w_t, gate_bias, conv_w, hng, upper, w_out,
      norm2_g, w_gate, w_up, w_down, final_g, *state0)
    return y.reshape(bsz, t_len, D_MODEL), conv, cn, m


def _mixer_kernel(x_ref, g1_ref, wnk_ref, wt_ref, gbias_ref, convw_ref, hng_ref, ut_ref, wout_ref,
                  conv0_ref, cn0_ref, m0_ref, xo_ref, conv_ref, cn_ref, m_ref, *scr,
                  tt, chunk, t_valid, row_chunk):
    state = (conv0_ref[0], [cn0_ref[0, h] for h in range(HEADS)], m0_ref[0])
    conv_new, cn_new, m_new, _ = _mixer_tile(
        x_ref.at[0], (g1_ref, wnk_ref, wt_ref, gbias_ref, convw_ref, hng_ref, ut_ref, wout_ref),
        state, scr, xo_ref.at[0], tt=tt, chunk=chunk, t_valid=t_valid, row_chunk=row_chunk)
    conv_ref[0] = conv_new
    for h in range(HEADS):
        cn_ref[0, h] = cn_new[h]
    m_ref[0] = m_new


def _sample_mixer(x, mixer_w, state0, *, layer, t_valid, name):
    bsz, tt, _ = x.shape
    norm1_g, w_nk, w_t, gate_bias, conv_w, hng, w_out = mixer_w
    upper = _upper_ones(tt)
    kern = functools.partial(_mixer_kernel, tt=tt, chunk=tt, t_valid=t_valid, row_chunk=min(64, tt))
    seq_spec = lambda *tail: pl.BlockSpec((1,) + tail, lambda b: (b,) + (0,) * len(tail))
    state_specs = [seq_spec(CONV_WIDTH - 1, CONV_DIM), seq_spec(HEADS, STATE_ROWS, HEAD_DIM),
                   seq_spec(8, LANES)]
    ls = lambda arr: _layer_spec(arr, layer)
    return pl.pallas_call(
        kern,
        grid=(bsz,),
        in_specs=[seq_spec(tt, D_MODEL), ls(norm1_g), ls(w_nk), ls(w_t), ls(gate_bias), ls(conv_w),
                  ls(hng), _whole_spec(upper), ls(w_out)] + state_specs,
        out_specs=[seq_spec(tt, D_MODEL)] + state_specs,
        out_shape=(
            jax.ShapeDtypeStruct((bsz, tt, D_MODEL), F32),
            jax.ShapeDtypeStruct((bsz, CONV_WIDTH - 1, CONV_DIM), F32),
            jax.ShapeDtypeStruct((bsz, HEADS, STATE_ROWS, HEAD_DIM), F32),
            jax.ShapeDtypeStruct((bsz, 8, LANES), F32),
        ),
        scratch_shapes=_mixer_scratch(tt),
        compiler_params=pltpu.CompilerParams(
            dimension_semantics=("arbitrary",), vmem_limit_bytes=VMEM_LIMIT),
        name=name,
    )(x, norm1_g, w_nk, w_t, gate_bias, conv_w, hng, upper, w_out, *state0)


def _ffn_kernel(x_ref, g2_ref, wg_ref, wu_ref, wd_ref, gf_ref, y_ref, hb_ref, *, final_norm):
    _ffn_norm(x_ref[...], g2_ref, hb_ref)
    _ffn_matmuls(x_ref, hb_ref, wg_ref, wu_ref, wd_ref, gf_ref, y_ref, final_norm=final_norm)


def _sample_ffn(x, ffn_w, *, layer, final_norm, name):
    rows = x.shape[0]
    norm2_g, w_gate, w_up, w_down, final_g = ffn_w
    ls = lambda arr: _layer_spec(arr, layer)
    x_spec = pl.BlockSpec((rows, D_MODEL), lambda i: (0, 0))
    return pl.pallas_call(
        functools.partial(_ffn_kernel, final_norm=final_norm),
        grid=(1,),
        in_specs=[x_spec, ls(norm2_g), ls(w_gate), ls(w_up), ls(w_down), _whole_spec(final_g)],
        out_specs=x_spec,
        out_shape=jax.ShapeDtypeStruct((rows, D_MODEL), F32),
        scratch_shapes=[pltpu.VMEM((rows, D_MODEL), BF16)],
        compiler_params=pltpu.CompilerParams(
            dimension_semantics=("arbitrary",), vmem_limit_bytes=VMEM_LIMIT),
        name=name,
    )(x, norm2_g, w_gate, w_up, w_down, final_g)


PROMPT_TILE = 512
PROMPT_CHUNK = 256
SAMPLE_PAD = 128


def _pack_state(c, n):
    n_rows = jnp.broadcast_to(n[..., None, :], n.shape[:-1] + (STATE_ROWS - HEAD_DIM, HEAD_DIM))
    return jnp.concatenate([jnp.swapaxes(c, -1, -2), n_rows], axis=-2)


def _pack_m(m):
    m8 = jnp.pad(m, ((0, 0),) * (m.ndim - 1) + ((0, 8 - HEADS),))
    return jnp.broadcast_to(m8[..., None], m8.shape + (LANES,))


def _unpack_state(cn, m):
    return (jnp.swapaxes(cn[..., 0:HEAD_DIM, :], -1, -2), cn[..., HEAD_DIM, :], m[..., 0:HEADS, 0])


def kernel(x_prompt, x_sample, state_conv, state_mlstm_C, state_mlstm_n, state_mlstm_m, norm1_g, w_mix_in, conv_w, b_igate, b_fgate, head_norm_g, w_mix_out, norm2_g, w_gate, w_up, w_down, final_norm_g):
    depth = w_mix_in.shape[0]
    bp = x_prompt.shape[0]
    bs, ts, _ = x_sample.shape

    hp = x_prompt
    hs = x_sample
    final_g = final_norm_g.reshape(1, D_MODEL)

    zeros_p = (jnp.zeros((bp, CONV_WIDTH - 1, CONV_DIM), F32),
               jnp.zeros((bp, HEADS, STATE_ROWS, HEAD_DIM), F32),
               jnp.zeros((bp, 8, LANES), F32))

    c_q = 3 * CONV_DIM
    c_k, c_v, c_o, c_g = c_q + MLSTM_DIM, c_q + 2 * MLSTM_DIM, c_q + 3 * MLSTM_DIM, c_q + 4 * MLSTM_DIM

    w = w_mix_in
    w_nk = jnp.concatenate([w[:, :, 0:c_q], w[:, :, c_k:c_v]], axis=2).astype(BF16)
    gate_cols = jnp.zeros((depth, D_MODEL, GATE_ROWS), F32)
    gate_cols = gate_cols.at[:, :, 0:HEADS].set(w[:, :, c_g:c_g + HEADS])
    gate_cols = gate_cols.at[:, :, 8:8 + HEADS].set(w[:, :, c_g + HEADS:c_g + 2 * HEADS])
    w_t = jnp.swapaxes(jnp.concatenate(
        [gate_cols, w[:, :, c_q:c_k], w[:, :, c_v:c_o], w[:, :, c_o:c_g]], axis=2).astype(BF16), 1, 2)
    gate_bias = jnp.zeros((depth, GATE_ROWS), F32)
    gate_bias = gate_bias.at[:, 0:HEADS].set(b_igate).at[:, 8:8 + HEADS].set(b_fgate)
    gate_bias = jnp.broadcast_to(gate_bias[:, :, None], (depth, GATE_ROWS, LANES))
    hng = jnp.broadcast_to(head_norm_g.reshape(depth, MLSTM_DIM, 1), (depth, MLSTM_DIM, LANES))
    mixer_w = (norm1_g.reshape(depth, 1, D_MODEL), w_nk, w_t, gate_bias, conv_w, hng,
               w_mix_out.astype(BF16))
    ffn_w = (norm2_g.reshape(depth, 1, D_MODEL), w_gate.astype(BF16), w_up.astype(BF16),
             w_down.astype(BF16), final_g)
    cn_s0 = _pack_state(state_mlstm_C, state_mlstm_n)
    m_s0 = _pack_m(state_mlstm_m)

    outs_p, outs_s = [], []
    for l in range(depth):
        last = l == depth - 1
        hp, conv_p, cn_p, m_p = _prompt_layer(
            hp, mixer_w, ffn_w, zeros_p, layer=l, tt=PROMPT_TILE, chunk=PROMPT_CHUNK,
            final_norm=last, name=f"layer_prompt_{l}")
        outs_p.append((conv_p, cn_p, m_p))

        hs_pad = jnp.pad(hs, ((0, 0), (0, SAMPLE_PAD - ts), (0, 0)))
        hs_pad, conv_s, cn_s, m_s = _sample_mixer(
            hs_pad, mixer_w, (state_conv[l], cn_s0[l], m_s0[l]), layer=l, t_valid=ts,
            name=f"mixer_sample_{l}")
        hs = _sample_ffn(hs_pad[:, :ts].reshape(bs * ts, D_MODEL), ffn_w, layer=l,
                         final_norm=last, name=f"ffn_sample_{l}").reshape(bs, ts, D_MODEL)
        outs_s.append((conv_s, cn_s, m_s))

    def states(outs):
        conv = jnp.stack([o[0] for o in outs])
        c, n, m = _unpack_state(jnp.stack([o[1] for o in outs]), jnp.stack([o[2] for o in outs]))
        return conv, c, n, m

    return (hp, hs) + states(outs_p) + states(outs_s)
```

```python
import functools

import jax
import jax.numpy as jnp
from jax import lax
from jax.experimental import pallas as pl
from jax.experimental.pallas import tpu as pltpu

D_MODEL = 1024
CONV_DIM = 512
CONV_WIDTH = 3
HEADS = 4
HEAD_DIM = 128
MLSTM_DIM = HEADS * HEAD_DIM
MIX_DIM = CONV_DIM + MLSTM_DIM
D_FF = 2816
EPS = 1e-6

LANES = 128
BF16_ROWS = 16
MXU_WIDTH = 256
STATE_ROWS = HEAD_DIM + BF16_ROWS
GATE_ROWS = 2 * 8
ROW_V = GATE_ROWS + MLSTM_DIM
ROW_O = ROW_V + MLSTM_DIM
FFN_SPLIT = (D_FF // MXU_WIDTH + 1) // 2 * MXU_WIDTH
VMEM_LIMIT = 60 * 1024 * 1024

BF16 = jnp.bfloat16
F32 = jnp.float32
_NT = (((1,), (1,)), ((), ()))


def _rms_norm(x, g):
    ms = jnp.mean(x * x, axis=-1, keepdims=True)
    return x * lax.rsqrt(ms + EPS) * g


def _sigmoid(x):
    return 1.0 / (1.0 + jnp.exp(-x))


def _log_sigmoid(x):
    return jnp.minimum(x, 0.0) - jnp.log1p(jnp.exp(-jnp.abs(x)))


def _dot(a, b):
    return jnp.dot(a, b, preferred_element_type=F32)


def _mixer_tile(x_ref, w, state, scr, xo_ref, *, tt, chunk, t_valid, row_chunk):
    g1_ref, wnk_ref, wt_ref, gbias_ref, convw_ref, hng_ref, ut_ref, wout_ref = w
    conv_prev, cn_prev, m_start = state
    hb_ref, pc_ref, k_ref, qt_ref, vt_ref, ot_ref, mix_ref = scr

    hb_ref[...] = _rms_norm(x_ref[...], g1_ref[...]).astype(BF16)
    n_lane_tiles = tt // LANES

    gq = lax.dot_general(wt_ref[0:ROW_V, :], hb_ref[...], _NT, preferred_element_type=F32)
    qt_ref[...] = gq[GATE_ROWS:, :].astype(BF16)
    graw = gq[0:GATE_ROWS, :] + jnp.concatenate([gbias_ref[...]] * n_lane_tiles, axis=1)
    li = graw[0:8]
    lf = _log_sigmoid(graw[8:16])
    p1 = lf.astype(BF16).astype(F32)
    p2 = (lf - p1).astype(BF16).astype(F32)
    p3 = lf - p1 - p2
    pieces = jnp.concatenate([p1, p2, p3, jnp.zeros_like(p1)], axis=0).astype(BF16)
    csum = _dot(pieces, ut_ref[...])
    bt = csum[0:8] + csum[8:16] + csum[16:24]
    a = li - bt
    lane_t = lax.broadcasted_iota(jnp.int32, (8, tt), 1)
    cmax = a
    shift = 1
    while shift < tt:
        cmax = jnp.maximum(cmax, jnp.where(lane_t >= shift, pltpu.roll(cmax, shift, axis=1),
                                           -jnp.inf))
        shift *= 2
    mt = bt + jnp.maximum(jnp.concatenate([m_start] * n_lane_tiles, axis=1), cmax)
    u = bt - mt
    eneg = jnp.exp(-mt)
    a_cols = jnp.concatenate([a, jnp.zeros((LANES - 8, tt), F32)], axis=0).T
    m_new = jnp.broadcast_to(mt[:, t_valid - 1:t_valid], (8, LANES))

    pc_ref[...] = _dot(hb_ref[...], wnk_ref[:, 0:3 * CONV_DIM])
    w0 = convw_ref[0:1, :]
    w1 = convw_ref[1:2, :]
    w2 = convw_ref[2:3, :]
    row_c = lax.broadcasted_iota(jnp.int32, (row_chunk, CONV_DIM), 0)
    prev2, prev1 = conv_prev[0:1, :], conv_prev[1:2, :]
    for r0 in range(0, tt, row_chunk):
        rows = slice(r0, r0 + row_chunk)
        z = pc_ref[rows, 2 * CONV_DIM:3 * CONV_DIM] * pc_ref[rows, 0:CONV_DIM]
        z1 = jnp.where(row_c == 0, prev1, pltpu.roll(z, 1, axis=0))
        z2 = jnp.where(row_c == 0, prev2, jnp.where(row_c == 1, prev1, pltpu.roll(z, 2, axis=0)))
        y = z2 * w0 + z1 * w1 + z * w2
        mix_ref[rows, 0:CONV_DIM] = (pc_ref[rows, CONV_DIM:2 * CONV_DIM] * y).astype(BF16)
        prev2, prev1 = z[row_chunk - 2:row_chunk - 1, :], z[row_chunk - 1:row_chunk, :]
        if r0 <= t_valid - 2 < r0 + row_chunk:
            conv_new = z[t_valid - 2 - r0:t_valid - r0, :]

    k_ref[...] = (_dot(hb_ref[...], wnk_ref[:, 3 * CONV_DIM:3 * CONV_DIM + MLSTM_DIM])
                  * HEAD_DIM ** -0.5).astype(BF16)
    vt = lax.dot_general(wt_ref[ROW_V:ROW_O, :], hb_ref[...], _NT, preferred_element_type=F32)
    for h in range(HEADS):
        vt_ref[h, 0:HEAD_DIM, :] = vt[h * HEAD_DIM:(h + 1) * HEAD_DIM, :].astype(BF16)
        vt_ref[h, HEAD_DIM:STATE_ROWS, :] = jnp.ones((STATE_ROWS - HEAD_DIM, tt), BF16)
    ot_ref[...] = _sigmoid(lax.dot_general(wt_ref[ROW_O:ROW_O + MLSTM_DIM, :], hb_ref[...], _NT,
                                           preferred_element_type=F32))

    causal_t = (lax.broadcasted_iota(jnp.int32, (chunk, chunk), 0)
                <= lax.broadcasted_iota(jnp.int32, (chunk, chunk), 1))
    lane_c = lax.broadcasted_iota(jnp.int32, (8, chunk), 1)
    c_lane_tiles = chunk // LANES

    cn_cur = list(cn_prev)
    for c in range(tt // chunk):
        cs = c * chunk
        lv = min(chunk, t_valid - cs)
        if lv <= 0:
            break
        ce = cs + lv - 1
        rows = slice(cs, cs + chunk)
        if c == 0:
            kappa = jnp.concatenate([m_start] * c_lane_tiles, axis=1)
        else:
            kappa = jnp.broadcast_to(-u[:, cs - 1:cs], (8, chunk))
        u_end = u[:, ce:ce + 1]
        w_carry = jnp.exp(u[:, rows] + kappa)
        w_state = jnp.exp(a[:, rows] + jnp.broadcast_to(u_end, (8, chunk)))
        if lv < chunk:
            w_state = jnp.where(lane_c < lv, w_state, 0.0)
        decay = jnp.exp(jnp.broadcast_to(u_end, (8, LANES)) + kappa[:, 0:LANES])
        for h in range(HEADS):
            hcols = slice(h * HEAD_DIM, (h + 1) * HEAD_DIM)
            k = k_ref[rows, hcols]
            q_t = qt_ref[hcols, rows]
            v_t = vt_ref[h, :, rows]
            w_t = jnp.exp(jnp.where(causal_t, a_cols[rows, h:h + 1] + u[h:h + 1, rows], -jnp.inf))
            s_t = (_dot(k, q_t) * w_t).astype(BF16)
            cn = cn_cur[h]
            tot = w_carry[h:h + 1, :] * _dot(cn.astype(BF16), q_t) + _dot(v_t, s_t)
            den = tot[HEAD_DIM:HEAD_DIM + 1, :]
            inv = 1.0 / jnp.maximum(jnp.abs(den), eneg[h:h + 1, rows])
            h_t = tot[0:HEAD_DIM, :] * inv
            mu = jnp.mean(h_t, axis=0, keepdims=True)
            d = h_t - mu
            var = jnp.mean(d * d, axis=0, keepdims=True)
            g = jnp.concatenate([hng_ref[hcols, :]] * c_lane_tiles, axis=1)
            y_t = d * lax.rsqrt(var + EPS) * g * ot_ref[hcols, rows]
            mix_ref[rows, CONV_DIM + h * HEAD_DIM:CONV_DIM + (h + 1) * HEAD_DIM] = (
                y_t.T.astype(BF16))
            v_w = (v_t.astype(F32) * w_state[h:h + 1, :]).astype(BF16)
            cn_cur[h] = decay[h:h + 1, :] * cn + _dot(v_w, k)

    xo = x_ref[...] + _dot(mix_ref[...], wout_ref[...])
    xo_ref[...] = xo
    return conv_new, cn_cur, m_new, xo


def _ffn_norm(x, g2_ref, hb_ref):
    hb_ref[...] = _rms_norm(x, g2_ref[...]).astype(BF16)


def _ffn_matmuls(x_ref, hb_ref, wg_ref, wu_ref, wd_ref, gf_ref, y_ref, *, final_norm):
    parts = ((0, FFN_SPLIT), (FFN_SPLIT, D_FF))
    for i, (f0, f1) in enumerate(parts):
        g = _dot(hb_ref[...], wg_ref[:, f0:f1])
        u = _dot(hb_ref[...], wu_ref[:, f0:f1])
        a = (g * _sigmoid(g) * u).astype(BF16)
        base = x_ref if i == 0 else y_ref
        acc = base[...] + _dot(a, wd_ref[f0:f1, :])
        if final_norm and i == len(parts) - 1:
            acc = _rms_norm(acc, gf_ref[...])
        y_ref[...] = acc


def _mixer_scratch(tt):
    return [
        pltpu.VMEM((tt, D_MODEL), BF16),
        pltpu.VMEM((tt, 3 * CONV_DIM), F32),
        pltpu.VMEM((tt, MLSTM_DIM), BF16),
        pltpu.VMEM((MLSTM_DIM, tt), BF16),
        pltpu.VMEM((HEADS, STATE_ROWS, tt), BF16),
        pltpu.VMEM((MLSTM_DIM, tt), F32),
        pltpu.VMEM((tt, MIX_DIM), BF16),
    ]


def _upper_ones(tt):
    return (lax.broadcasted_iota(jnp.int32, (tt, tt), 0)
            <= lax.broadcasted_iota(jnp.int32, (tt, tt), 1)).astype(BF16)


def _layer_spec(stacked, layer):
    tail = stacked.shape[1:]
    return pl.BlockSpec((None,) + tail, lambda *_: (layer,) + (0,) * len(tail),
                        pipeline_mode=pl.Buffered(1))


def _whole_spec(arr):
    return pl.BlockSpec(arr.shape, lambda *_: (0,) * arr.ndim, pipeline_mode=pl.Buffered(1))


def _layer_kernel(x_ref, g1_ref, wnk_ref, wt_ref, gbias_ref, convw_ref, hng_ref, ut_ref, wout_ref,
                  g2_ref, wg_ref, wu_ref, wd_ref, gf_ref,
                  y_ref, conv_ref, c_ref, n_ref, m_ref,
                  hb_ref, pc_ref, k_ref, qt_ref, vt_ref, ot_ref, mix_ref, x1_ref, hb2_ref,
                  conv_s, cn_s, m_s,
                  *, tt, chunk, row_chunk, tiles_per_seq, n_tiles, final_norm):
    g = pl.program_id(0)
    tile_in_seq = lax.rem(g, tiles_per_seq)

    @pl.when(g == 0)
    def _():
        x1_ref[...] = jnp.zeros_like(x1_ref)

    @pl.when(tile_in_seq == 0)
    def _():
        conv_s[...] = jnp.zeros_like(conv_s)
        cn_s[...] = jnp.zeros_like(cn_s)
        m_s[...] = jnp.zeros_like(m_s)

    x1 = x1_ref[...]
    _ffn_norm(x1, g2_ref, hb2_ref)
    y_ref[...] = x1

    state = (conv_s[...], [cn_s[h] for h in range(HEADS)], m_s[...])
    conv_new, cn_new, m_new, _ = _mixer_tile(
        x_ref, (g1_ref, wnk_ref, wt_ref, gbias_ref, convw_ref, hng_ref, ut_ref, wout_ref), state,
        (hb_ref, pc_ref, k_ref, qt_ref, vt_ref, ot_ref, mix_ref), x1_ref,
        tt=tt, chunk=chunk, t_valid=tt, row_chunk=row_chunk)
    _ffn_matmuls(y_ref, hb2_ref, wg_ref, wu_ref, wd_ref, gf_ref, y_ref, final_norm=final_norm)
    conv_s[...] = conv_new
    for h in range(HEADS):
        cn_s[h] = cn_new[h]
    m_s[...] = m_new

    @pl.when(jnp.logical_and(tile_in_seq == tiles_per_seq - 1, g < n_tiles))
    def _():
        conv_ref[0] = conv_s[...]
        for h in range(HEADS):
            c_ref[0, h] = cn_s[h, 0:HEAD_DIM, :].T
            n_ref[0, h:h + 1, :] = cn_s[h, HEAD_DIM:HEAD_DIM + 1, :]
        m_ref[0] = m_s[...]


def _prompt_layer(x, mixer_w, ffn_w, *, layer, tt, chunk, final_norm, name):
    bsz, t_len, _ = x.shape
    tiles_per_seq = t_len // tt
    n_tiles = bsz * tiles_per_seq
    norm1_g, w_nk, w_t, gate_bias, conv_w, hng, w_out = mixer_w
    norm2_g, w_gate, w_up, w_down, final_g = ffn_w
    upper = _upper_ones(tt)
    kern = functools.partial(_layer_kernel, tt=tt, chunk=chunk, row_chunk=min(64, tt),
                             tiles_per_seq=tiles_per_seq, n_tiles=n_tiles, final_norm=final_norm)
    mixer_tile = lambda g: jnp.minimum(g, n_tiles - 1)
    seq_spec = lambda *tail: pl.BlockSpec(
        (1,) + tail, lambda g: (mixer_tile(g) // tiles_per_seq,) + (0,) * len(tail))
    state_specs = [seq_spec(CONV_WIDTH - 1, CONV_DIM), seq_spec(HEADS, HEAD_DIM, HEAD_DIM),
                   seq_spec(HEADS, HEAD_DIM), seq_spec(8, LANES)]
    ls = lambda arr: _layer_spec(arr, layer)
    y, conv, c, n, m = pl.pallas_call(
        kern,
        grid=(n_tiles + 1,),
        in_specs=[pl.BlockSpec((tt, D_MODEL), lambda g: (mixer_tile(g), 0)),
                  ls(norm1_g), ls(w_nk), ls(w_t), ls(gate_bias), ls(conv_w), ls(hng),
                  _whole_spec(upper), ls(w_out),
                  ls(norm2_g), ls(w_gate), ls(w_up), ls(w_down), _whole_spec(final_g)],
        out_specs=[pl.BlockSpec((tt, D_MODEL), lambda g: (jnp.maximum(g - 1, 0), 0))] + state_specs,
        out_shape=(
            jax.ShapeDtypeStruct((bsz * t_len, D_MODEL), F32),
            jax.ShapeDtypeStruct((bsz, CONV_WIDTH - 1, CONV_DIM), F32),
            jax.ShapeDtypeStruct((bsz, HEADS, HEAD_DIM, HEAD_DIM), F32),
            jax.ShapeDtypeStruct((bsz, HEADS, HEAD_DIM), F32),
            jax.ShapeDtypeStruct((bsz, 8, LANES), F32),
        ),
        scratch_shapes=_mixer_scratch(tt) + [
            pltpu.VMEM((tt, D_MODEL), F32),
            pltpu.VMEM((tt, D_MODEL), BF16),
            pltpu.VMEM((CONV_WIDTH - 1, CONV_DIM), F32),
            pltpu.VMEM((HEADS, STATE_ROWS, HEAD_DIM), F32),
            pltpu.VMEM((8, LANES), F32),
        ],
        compiler_params=pltpu.CompilerParams(
            dimension_semantics=("arbitrary",), vmem_limit_bytes=VMEM_LIMIT),
        name=name,
    )(x.reshape(bsz * t_len, D_MODEL), norm1_g, w_nk, w_t, gate_bias, conv_w, hng, upper, w_out,
      norm2_g, w_gate, w_up, w_down, final_g)
    return y.reshape(bsz, t_len, D_MODEL), conv, c, n, m[:, 0:HEADS, 0]


def _rows_dot_exact(val, mat_ref):
    p1 = val.astype(BF16).astype(F32)
    p2 = (val - p1).astype(BF16).astype(F32)
    p3 = val - p1 - p2
    pieces = jnp.concatenate([p1, p2, p3, jnp.zeros_like(p1)], axis=0).astype(BF16)
    out = _dot(pieces, mat_ref[...])
    return out[0:8] + out[8:16] + out[16:24]


def _sample_layer_kernel(x_ref, g1_ref, wnk_ref, wt_ref, gbias_ref, convw_ref, hng_ref, segut_ref,
                         selend_ref, wout_ref, g2_ref, wg_ref, wu_ref, wd_ref, gf_ref,
                         prev1_ref, prev2_ref, cn0_ref, m0_ref,
                         y_ref, z_ref, cn_ref, mt_ref,
                         hb_ref, pc_ref, k_ref, qt_ref, vt_ref, ot_ref, mix_ref, hb2_ref,
                         *, n_seq, seg, row_chunk, final_norm):
    rows_all = n_seq * seg
    shift_bits = seg.bit_length() - 1
    n_lane_tiles = rows_all // LANES
    hb_ref[...] = _rms_norm(x_ref[...], g1_ref[...]).astype(BF16)

    gq = lax.dot_general(wt_ref[0:ROW_V, :], hb_ref[...], _NT, preferred_element_type=F32)
    qt_ref[...] = gq[GATE_ROWS:, :].astype(BF16)
    graw = gq[0:GATE_ROWS, :] + jnp.concatenate([gbias_ref[...]] * n_lane_tiles, axis=1)
    li = graw[0:8]
    lf = _log_sigmoid(graw[8:16])
    bt = _rows_dot_exact(lf, segut_ref)
    a = li - bt
    frame = lax.broadcasted_iota(jnp.int32, (8, rows_all), 1) & (seg - 1)
    cmax = a
    shift = 1
    while shift < seg:
        cmax = jnp.maximum(cmax, jnp.where(frame >= shift, pltpu.roll(cmax, shift, axis=1), -jnp.inf))
        shift *= 2
    m0 = m0_ref[...]
    mt = bt + jnp.maximum(m0, cmax)
    mt_ref[...] = mt
    u = bt - mt
    eneg = jnp.exp(-mt)
    u_end = _rows_dot_exact(u, selend_ref)
    w_carry = jnp.exp(u + m0)
    w_state = jnp.exp(a + u_end)
    decay = jnp.exp(u_end + m0)
    pad_rows = jnp.zeros((LANES - 8, rows_all), F32)
    a_cols = jnp.concatenate([a, pad_rows], axis=0).T
    decay_cols = jnp.concatenate([decay, pad_rows], axis=0).T

    pc_ref[...] = _dot(hb_ref[...], wnk_ref[:, 0:3 * CONV_DIM])
    w0 = convw_ref[0:1, :]
    w1 = convw_ref[1:2, :]
    w2 = convw_ref[2:3, :]
    frame_c = lax.broadcasted_iota(jnp.int32, (row_chunk, CONV_DIM), 0) & (seg - 1)
    for r0 in range(0, rows_all, row_chunk):
        rows = slice(r0, r0 + row_chunk)
        z = pc_ref[rows, 2 * CONV_DIM:3 * CONV_DIM] * pc_ref[rows, 0:CONV_DIM]
        z_ref[rows, :] = z
        z1 = jnp.where(frame_c == 0, prev1_ref[rows, :], pltpu.roll(z, 1, axis=0))
        z2 = jnp.where(frame_c == 0, prev2_ref[rows, :],
                       jnp.where(frame_c == 1, prev1_ref[rows, :], pltpu.roll(z, 2, axis=0)))
        y = z2 * w0 + z1 * w1 + z * w2
        mix_ref[rows, 0:CONV_DIM] = (pc_ref[rows, CONV_DIM:2 * CONV_DIM] * y).astype(BF16)

    k_ref[...] = (_dot(hb_ref[...], wnk_ref[:, 3 * CONV_DIM:3 * CONV_DIM + MLSTM_DIM])
                  * HEAD_DIM ** -0.5).astype(BF16)
    vt = lax.dot_general(wt_ref[ROW_V:ROW_O, :], hb_ref[...], _NT, preferred_element_type=F32)
    for h in range(HEADS):
        vt_ref[h, 0:HEAD_DIM, :] = vt[h * HEAD_DIM:(h + 1) * HEAD_DIM, :].astype(BF16)
        vt_ref[h, HEAD_DIM:STATE_ROWS, :] = jnp.ones((STATE_ROWS - HEAD_DIM, rows_all), BF16)
    ot_ref[...] = _sigmoid(lax.dot_general(wt_ref[ROW_O:ROW_O + MLSTM_DIM, :], hb_ref[...], _NT,
                                           preferred_element_type=F32))

    row_i = lax.broadcasted_iota(jnp.int32, (rows_all, rows_all), 0)
    col_i = lax.broadcasted_iota(jnp.int32, (rows_all, rows_all), 1)
    causal_t = jnp.logical_and((row_i >> shift_bits) == (col_i >> shift_bits), row_i <= col_i)
    lane_seq = lax.broadcasted_iota(jnp.int32, (STATE_ROWS, rows_all), 1) >> shift_bits
    for h in range(HEADS):
        hcols = slice(h * HEAD_DIM, (h + 1) * HEAD_DIM)
        k = k_ref[:, hcols]
        q_t = qt_ref[hcols, :]
        v_t = vt_ref[h]
        w_t = jnp.exp(jnp.where(causal_t, a_cols[:, h:h + 1] + u[h:h + 1, :], -jnp.inf))
        s_t = (_dot(k, q_t) * w_t).astype(BF16)
        carried = _dot(cn0_ref[h].astype(BF16), q_t)
        own = jnp.zeros((STATE_ROWS, rows_all), F32)
        for s in range(n_seq):
            own = own + jnp.where(lane_seq == s, carried[s * STATE_ROWS:(s + 1) * STATE_ROWS], 0.0)
        tot = w_carry[h:h + 1, :] * own + _dot(v_t, s_t)
        den = tot[HEAD_DIM:HEAD_DIM + 1, :]
        inv = 1.0 / jnp.maximum(jnp.abs(den), eneg[h:h + 1, :])
        h_t = tot[0:HEAD_DIM, :] * inv
        mu = jnp.mean(h_t, axis=0, keepdims=True)
        d = h_t - mu
        var = jnp.mean(d * d, axis=0, keepdims=True)
        g = jnp.concatenate([hng_ref[hcols, :]] * n_lane_tiles, axis=1)
        y_t = d * lax.rsqrt(var + EPS) * g * ot_ref[hcols, :]
        mix_ref[:, CONV_DIM + h * HEAD_DIM:CONV_DIM + (h + 1) * HEAD_DIM] = y_t.T.astype(BF16)
        v_w = v_t.astype(F32) * w_state[h:h + 1, :]
        v_w_streams = jnp.concatenate(
            [jnp.where(lane_seq == s, v_w, 0.0).astype(BF16) for s in range(n_seq)], axis=0)
        update = _dot(v_w_streams, k)
        for s in range(n_seq):
            block = slice(s * STATE_ROWS, (s + 1) * STATE_ROWS)
            dec = jnp.broadcast_to(decay_cols[s * seg:s * seg + 1, h:h + 1], (STATE_ROWS, HEAD_DIM))
            cn_ref[h, block, :] = dec * cn0_ref[h, block, :] + update[block]

    xo = x_ref[...] + _dot(mix_ref[...], wout_ref[...])
    _ffn_norm(xo, g2_ref, hb2_ref)
    y_ref[...] = xo
    _ffn_matmuls(y_ref, hb2_ref, wg_ref, wu_ref, wd_ref, gf_ref, y_ref, final_norm=final_norm)


def _sample_layer(x, mixer_w, ffn_w, conv0, c0, n0, m0, *, layer, final_norm, name):
    bsz, seg, _ = x.shape
    rows = bsz * seg
    assert seg & (seg - 1) == 0 and seg >= CONV_WIDTH - 1 and rows % LANES == 0
    norm1_g, w_nk, w_t, gate_bias, conv_w, hng, w_out = mixer_w
    norm2_g, w_gate, w_up, w_down, final_g = ffn_w
    s_i = lax.broadcasted_iota(jnp.int32, (rows, rows), 0)
    t_i = lax.broadcasted_iota(jnp.int32, (rows, rows), 1)
    seg_upper = jnp.logical_and(s_i // seg == t_i // seg, s_i <= t_i).astype(BF16)
    sel_end = (s_i == (t_i // seg) * seg + seg - 1).astype(BF16)
    prev1 = jnp.repeat(conv0[:, 1, :], seg, axis=0)
    prev2 = jnp.repeat(conv0[:, 0, :], seg, axis=0)
    cn0 = jnp.swapaxes(_pack_state(c0, n0), 0, 1).reshape(HEADS, bsz * STATE_ROWS, HEAD_DIM)
    m0_rows = jnp.repeat(jnp.pad(m0.T, ((0, 8 - HEADS), (0, 0))), seg, axis=1)
    ls = lambda arr: _layer_spec(arr, layer)
    whole = lambda shape: pl.BlockSpec(shape, lambda i: (0,) * len(shape))
    kern = functools.partial(_sample_layer_kernel, n_seq=bsz, seg=seg, row_chunk=min(64, rows),
                             final_norm=final_norm)
    y, z, cn, mt = pl.pallas_call(
        kern,
        grid=(1,),
        in_specs=[whole((rows, D_MODEL)), ls(norm1_g), ls(w_nk), ls(w_t), ls(gate_bias), ls(conv_w),
                  ls(hng), _whole_spec(seg_upper), _whole_spec(sel_end), ls(w_out),
                  ls(norm2_g), ls(w_gate), ls(w_up), ls(w_down), _whole_spec(final_g),
                  whole(prev1.shape), whole(prev2.shape), whole(cn0.shape), whole(m0_rows.shape)],
        out_specs=[whole((rows, D_MODEL)), whole((rows, CONV_DIM)), whole(cn0.shape),
                   whole(m0_rows.shape)],
        out_shape=(
            jax.ShapeDtypeStruct((rows, D_MODEL), F32),
            jax.ShapeDtypeStruct((rows, CONV_DIM), F32),
            jax.ShapeDtypeStruct(cn0.shape, F32),
            jax.ShapeDtypeStruct(m0_rows.shape, F32),
        ),
        scratch_shapes=_mixer_scratch(rows) + [pltpu.VMEM((rows, D_MODEL), BF16)],
        compiler_params=pltpu.CompilerParams(
            dimension_semantics=("arbitrary",), vmem_limit_bytes=VMEM_LIMIT),
        name=name,
    )(x.reshape(rows, D_MODEL), norm1_g, w_nk, w_t, gate_bias, conv_w, hng, seg_upper, sel_end, w_out,
      norm2_g, w_gate, w_up, w_down, final_g, prev1, prev2, cn0, m0_rows)
    conv = z.reshape(bsz, seg, CONV_DIM)[:, seg - (CONV_WIDTH - 1):, :]
    cn = jnp.swapaxes(cn.reshape(HEADS, bsz, STATE_ROWS, HEAD_DIM), 0, 1)
    c_new, n_new = jnp.swapaxes(cn[:, :, 0:HEAD_DIM, :], -1, -2), cn[:, :, HEAD_DIM, :]
    m_new = mt[0:HEADS, seg - 1::seg].T
    return y.reshape(bsz, seg, D_MODEL), conv, c_new, n_new, m_new


PROMPT_TILE = 512
PROMPT_CHUNK = 512


def _pack_state(c, n):
    n_rows = jnp.broadcast_to(n[..., None, :], n.shape[:-1] + (STATE_ROWS - HEAD_DIM, HEAD_DIM))
    return jnp.concatenate([jnp.swapaxes(c, -1, -2), n_rows], axis=-2)


def kernel(x_prompt, x_sample, state_conv, state_mlstm_C, state_mlstm_n, state_mlstm_m, norm1_g, w_mix_in, conv_w, b_igate, b_fgate, head_norm_g, w_mix_out, norm2_g, w_gate, w_up, w_down, final_norm_g):
    depth = w_mix_in.shape[0]
    bp = x_prompt.shape[0]
    bs, ts, _ = x_sample.shape

    hp = x_prompt
    hs = x_sample
    final_g = final_norm_g.reshape(1, D_MODEL)

    c_q = 3 * CONV_DIM
    c_k, c_v, c_o, c_g = c_q + MLSTM_DIM, c_q + 2 * MLSTM_DIM, c_q + 3 * MLSTM_DIM, c_q + 4 * MLSTM_DIM

    w = w_mix_in
    w_nk = jnp.concatenate([w[:, :, 0:c_q], w[:, :, c_k:c_v]], axis=2).astype(BF16)
    gate_cols = jnp.zeros((depth, D_MODEL, GATE_ROWS), F32)
    gate_cols = gate_cols.at[:, :, 0:HEADS].set(w[:, :, c_g:c_g + HEADS])
    gate_cols = gate_cols.at[:, :, 8:8 + HEADS].set(w[:, :, c_g + HEADS:c_g + 2 * HEADS])
    w_t = jnp.swapaxes(jnp.concatenate(
        [gate_cols, w[:, :, c_q:c_k], w[:, :, c_v:c_o], w[:, :, c_o:c_g]], axis=2).astype(BF16), 1, 2)
    gate_bias = jnp.zeros((depth, GATE_ROWS), F32)
    gate_bias = gate_bias.at[:, 0:HEADS].set(b_igate).at[:, 8:8 + HEADS].set(b_fgate)
    gate_bias = jnp.broadcast_to(gate_bias[:, :, None], (depth, GATE_ROWS, LANES))
    hng = jnp.broadcast_to(head_norm_g.reshape(depth, MLSTM_DIM, 1), (depth, MLSTM_DIM, LANES))
    mixer_w = (norm1_g.reshape(depth, 1, D_MODEL), w_nk, w_t, gate_bias, conv_w, hng,
               w_mix_out.astype(BF16))
    ffn_w = (norm2_g.reshape(depth, 1, D_MODEL), w_gate.astype(BF16), w_up.astype(BF16),
             w_down.astype(BF16), final_g)

    outs_p, outs_s = [], []
    for l in range(depth):
        last = l == depth - 1
        hp, *state_p = _prompt_layer(
            hp, mixer_w, ffn_w, layer=l, tt=PROMPT_TILE, chunk=PROMPT_CHUNK,
            final_norm=last, name=f"layer_prompt_{l}")
        outs_p.append(state_p)

        hs, *state_s = _sample_layer(
            hs, mixer_w, ffn_w, state_conv[l], state_mlstm_C[l], state_mlstm_n[l], state_mlstm_m[l],
            layer=l, final_norm=last, name=f"layer_sample_{l}")
        outs_s.append(state_s)

    stacked = lambda outs: tuple(jnp.stack([o[i] for o in outs]) for i in range(4))
    return (hp, hs) + stacked(outs_p) + stacked(outs_s)
```

```python
import functools

import jax
import jax.numpy as jnp
from jax import lax
from jax.experimental import pallas as pl
from jax.experimental.pallas import tpu as pltpu

D_MODEL = 1024
CONV_DIM = 512
CONV_WIDTH = 3
HEADS = 4
HEAD_DIM = 128
MLSTM_DIM = HEADS * HEAD_DIM
MIX_DIM = CONV_DIM + MLSTM_DIM
D_FF = 2816
EPS = 1e-6

LANES = 128
BF16_ROWS = 16
MXU_WIDTH = 256
STATE_ROWS = HEAD_DIM + BF16_ROWS
GATE_ROWS = 2 * 8
ROW_V = GATE_ROWS + MLSTM_DIM
ROW_O = ROW_V + MLSTM_DIM
FFN_SPLIT = (D_FF // MXU_WIDTH + 1) // 2 * MXU_WIDTH
VMEM_LIMIT = 60 * 1024 * 1024

BF16 = jnp.bfloat16
F32 = jnp.float32
_NT = (((1,), (1,)), ((), ()))


def _rms_norm(x, g):
    ms = jnp.mean(x * x, axis=-1, keepdims=True)
    return x * lax.rsqrt(ms + EPS) * g


def _sigmoid(x):
    return 1.0 / (1.0 + jnp.exp(-x))


def _log_sigmoid(x):
    return jnp.minimum(x, 0.0) - jnp.log1p(jnp.exp(-jnp.abs(x)))


def _dot(a, b):
    return jnp.dot(a, b, preferred_element_type=F32)


def _mixer_tile(x_ref, w, state, scr, xo_ref, *, tt, chunk, t_valid, row_chunk):
    g1_ref, wnk_ref, wt_ref, gbias_ref, convw_ref, hng_ref, ut_ref, wout_ref = w
    conv_prev, cn_prev, m_start = state
    hb_ref, pc_ref, k_ref, qt_ref, vt_ref, ot_ref, mix_ref = scr

    hb_ref[...] = _rms_norm(x_ref[...], g1_ref[...]).astype(BF16)
    n_lane_tiles = tt // LANES

    gq = lax.dot_general(wt_ref[0:ROW_V, :], hb_ref[...], _NT, preferred_element_type=F32)
    qt_ref[...] = gq[GATE_ROWS:, :].astype(BF16)
    graw = gq[0:GATE_ROWS, :] + jnp.concatenate([gbias_ref[...]] * n_lane_tiles, axis=1)
    li = graw[0:8]
    lf = _log_sigmoid(graw[8:16])
    p1 = lf.astype(BF16).astype(F32)
    p2 = (lf - p1).astype(BF16).astype(F32)
    p3 = lf - p1 - p2
    pieces = jnp.concatenate([p1, p2, p3, jnp.zeros_like(p1)], axis=0).astype(BF16)
    csum = _dot(pieces, ut_ref[...])
    bt = csum[0:8] + csum[8:16] + csum[16:24]
    a = li - bt
    lane_t = lax.broadcasted_iota(jnp.int32, (8, tt), 1)
    cmax = a
    shift = 1
    while shift < tt:
        cmax = jnp.maximum(cmax, jnp.where(lane_t >= shift, pltpu.roll(cmax, shift, axis=1),
                                           -jnp.inf))
        shift *= 2
    mt = bt + jnp.maximum(jnp.concatenate([m_start] * n_lane_tiles, axis=1), cmax)
    u = bt - mt
    eneg = jnp.exp(-mt)
    a_cols = jnp.concatenate([a, jnp.zeros((LANES - 8, tt), F32)], axis=0).T
    m_new = jnp.broadcast_to(mt[:, t_valid - 1:t_valid], (8, LANES))

    pc_ref[...] = _dot(hb_ref[...], wnk_ref[:, 0:3 * CONV_DIM])
    w0 = convw_ref[0:1, :]
    w1 = convw_ref[1:2, :]
    w2 = convw_ref[2:3, :]
    row_c = lax.broadcasted_iota(jnp.int32, (row_chunk, CONV_DIM), 0)
    prev2, prev1 = conv_prev[0:1, :], conv_prev[1:2, :]
    for r0 in range(0, tt, row_chunk):
        rows = slice(r0, r0 + row_chunk)
        z = pc_ref[rows, 2 * CONV_DIM:3 * CONV_DIM] * pc_ref[rows, 0:CONV_DIM]
        z1 = jnp.where(row_c == 0, prev1, pltpu.roll(z, 1, axis=0))
        z2 = jnp.where(row_c == 0, prev2, jnp.where(row_c == 1, prev1, pltpu.roll(z, 2, axis=0)))
        y = z2 * w0 + z1 * w1 + z * w2
        mix_ref[rows, 0:CONV_DIM] = (pc_ref[rows, CONV_DIM:2 * CONV_DIM] * y).astype(BF16)
        prev2, prev1 = z[row_chunk - 2:row_chunk - 1, :], z[row_chunk - 1:row_chunk, :]
        if r0 <= t_valid - 2 < r0 + row_chunk:
            conv_new = z[t_valid - 2 - r0:t_valid - r0, :]

    k_ref[...] = (_dot(hb_ref[...], wnk_ref[:, 3 * CONV_DIM:3 * CONV_DIM + MLSTM_DIM])
                  * HEAD_DIM ** -0.5).astype(BF16)
    vt = lax.dot_general(wt_ref[ROW_V:ROW_O, :], hb_ref[...], _NT, preferred_element_type=F32)
    for h in range(HEADS):
        vt_ref[h, 0:HEAD_DIM, :] = vt[h * HEAD_DIM:(h + 1) * HEAD_DIM, :].astype(BF16)
        vt_ref[h, HEAD_DIM:STATE_ROWS, :] = jnp.ones((STATE_ROWS - HEAD_DIM, tt), BF16)
    ot_ref[...] = _sigmoid(lax.dot_general(wt_ref[ROW_O:ROW_O + MLSTM_DIM, :], hb_ref[...], _NT,
                                           preferred_element_type=F32))

    causal_t = (lax.broadcasted_iota(jnp.int32, (chunk, chunk), 0)
                <= lax.broadcasted_iota(jnp.int32, (chunk, chunk), 1))
    lane_c = lax.broadcasted_iota(jnp.int32, (8, chunk), 1)
    c_lane_tiles = chunk // LANES

    cn_cur = list(cn_prev)
    for c in range(tt // chunk):
        cs = c * chunk
        lv = min(chunk, t_valid - cs)
        if lv <= 0:
            break
        ce = cs + lv - 1
        rows = slice(cs, cs + chunk)
        if c == 0:
            kappa = jnp.concatenate([m_start] * c_lane_tiles, axis=1)
        else:
            kappa = jnp.broadcast_to(-u[:, cs - 1:cs], (8, chunk))
        u_end = u[:, ce:ce + 1]
        w_carry = jnp.exp(u[:, rows] + kappa)
        w_state = jnp.exp(a[:, rows] + jnp.broadcast_to(u_end, (8, chunk)))
        if lv < chunk:
            w_state = jnp.where(lane_c < lv, w_state, 0.0)
        decay = jnp.exp(jnp.broadcast_to(u_end, (8, LANES)) + kappa[:, 0:LANES])
        for h in range(HEADS):
            hcols = slice(h * HEAD_DIM, (h + 1) * HEAD_DIM)
            k = k_ref[rows, hcols]
            q_t = qt_ref[hcols, rows]
            v_t = vt_ref[h, :, rows]
            w_t = jnp.exp(jnp.where(causal_t, a_cols[rows, h:h + 1] + u[h:h + 1, rows], -jnp.inf))
            s_t = (_dot(k, q_t) * w_t).astype(BF16)
            cn = cn_cur[h]
            tot = w_carry[h:h + 1, :] * _dot(cn.astype(BF16), q_t) + _dot(v_t, s_t)
            den = tot[HEAD_DIM:HEAD_DIM + 1, :]
            inv = 1.0 / jnp.maximum(jnp.abs(den), eneg[h:h + 1, rows])
            h_t = tot[0:HEAD_DIM, :] * inv
            mu = jnp.mean(h_t, axis=0, keepdims=True)
            d = h_t - mu
            var = jnp.mean(d * d, axis=0, keepdims=True)
            g = jnp.concatenate([hng_ref[hcols, :]] * c_lane_tiles, axis=1)
            y_t = d * lax.rsqrt(var + EPS) * g * ot_ref[hcols, rows]
            mix_ref[rows, CONV_DIM + h * HEAD_DIM:CONV_DIM + (h + 1) * HEAD_DIM] = (
                y_t.T.astype(BF16))
            v_w = (v_t.astype(F32) * w_state[h:h + 1, :]).astype(BF16)
            cn_cur[h] = decay[h:h + 1, :] * cn + _dot(v_w, k)

    xo = x_ref[...] + _dot(mix_ref[...], wout_ref[...])
    xo_ref[...] = xo
    return conv_new, cn_cur, m_new, xo


def _ffn_norm(x, g2_ref, hb_ref):
    hb_ref[...] = _rms_norm(x, g2_ref[...]).astype(BF16)


def _ffn_matmuls(x_ref, hb_ref, wg_ref, wu_ref, wd_ref, gf_ref, y_ref, *, final_norm):
    parts = ((0, FFN_SPLIT), (FFN_SPLIT, D_FF))
    for i, (f0, f1) in enumerate(parts):
        g = _dot(hb_ref[...], wg_ref[:, f0:f1])
        u = _dot(hb_ref[...], wu_ref[:, f0:f1])
        a = (g * _sigmoid(g) * u).astype(BF16)
        base = x_ref if i == 0 else y_ref
        acc = base[...] + _dot(a, wd_ref[f0:f1, :])
        if final_norm and i == len(parts) - 1:
            acc = _rms_norm(acc, gf_ref[...])
        y_ref[...] = acc


def _mixer_scratch(tt):
    return [
        pltpu.VMEM((tt, D_MODEL), BF16),
        pltpu.VMEM((tt, 3 * CONV_DIM), F32),
        pltpu.VMEM((tt, MLSTM_DIM), BF16),
        pltpu.VMEM((MLSTM_DIM, tt), BF16),
        pltpu.VMEM((HEADS, STATE_ROWS, tt), BF16),
        pltpu.VMEM((MLSTM_DIM, tt), F32),
        pltpu.VMEM((tt, MIX_DIM), BF16),
    ]


def _upper_ones(tt):
    return (lax.broadcasted_iota(jnp.int32, (tt, tt), 0)
            <= lax.broadcasted_iota(jnp.int32, (tt, tt), 1)).astype(BF16)


def _layer_spec(stacked, layer):
    tail = stacked.shape[1:]
    return pl.BlockSpec((None,) + tail, lambda *_: (layer,) + (0,) * len(tail),
                        pipeline_mode=pl.Buffered(1))


def _whole_spec(arr):
    return pl.BlockSpec(arr.shape, lambda *_: (0,) * arr.ndim, pipeline_mode=pl.Buffered(1))


def _layer_kernel(x_ref, g1_ref, wnk_ref, wt_ref, gbias_ref, convw_ref, hng_ref, ut_ref, wout_ref,
                  g2_ref, wg_ref, wu_ref, wd_ref, gf_ref,
                  y_ref, conv_ref, c_ref, n_ref, m_ref,
                  hb_ref, pc_ref, k_ref, qt_ref, vt_ref, ot_ref, mix_ref, x1_ref, hb2_ref,
                  conv_s, cn_s, m_s,
                  *, tt, chunk, row_chunk, tiles_per_seq, n_tiles, final_norm):
    g = pl.program_id(0)
    tile_in_seq = lax.rem(g, tiles_per_seq)

    @pl.when(g == 0)
    def _():
        x1_ref[...] = jnp.zeros_like(x1_ref)

    @pl.when(tile_in_seq == 0)
    def _():
        conv_s[...] = jnp.zeros_like(conv_s)
        cn_s[...] = jnp.zeros_like(cn_s)
        m_s[...] = jnp.zeros_like(m_s)

    x1 = x1_ref[...]
    _ffn_norm(x1, g2_ref, hb2_ref)
    y_ref[...] = x1

    state = (conv_s[...], [cn_s[h] for h in range(HEADS)], m_s[...])
    conv_new, cn_new, m_new, _ = _mixer_tile(
        x_ref, (g1_ref, wnk_ref, wt_ref, gbias_ref, convw_ref, hng_ref, ut_ref, wout_ref), state,
        (hb_ref, pc_ref, k_ref, qt_ref, vt_ref, ot_ref, mix_ref), x1_ref,
        tt=tt, chunk=chunk, t_valid=tt, row_chunk=row_chunk)
    _ffn_matmuls(y_ref, hb2_ref, wg_ref, wu_ref, wd_ref, gf_ref, y_ref, final_norm=final_norm)
    conv_s[...] = conv_new
    for h in range(HEADS):
        cn_s[h] = cn_new[h]
    m_s[...] = m_new

    @pl.when(jnp.logical_and(tile_in_seq == tiles_per_seq - 1, g < n_tiles))
    def _():
        conv_ref[0] = conv_s[...]
        for h in range(HEADS):
            c_ref[0, h] = cn_s[h, 0:HEAD_DIM, :].T
            n_ref[0, h:h + 1, :] = cn_s[h, HEAD_DIM:HEAD_DIM + 1, :]
        m_ref[0] = m_s[...]


def _prompt_layer(x, mixer_w, ffn_w, *, layer, tt, chunk, final_norm, name):
    bsz, t_len, _ = x.shape
    tiles_per_seq = t_len // tt
    n_tiles = bsz * tiles_per_seq
    norm1_g, w_nk, w_t, gate_bias, conv_w, hng, w_out = mixer_w
    norm2_g, w_gate, w_up, w_down, final_g = ffn_w
    upper = _upper_ones(tt)
    kern = functools.partial(_layer_kernel, tt=tt, chunk=chunk, row_chunk=min(64, tt),
                             tiles_per_seq=tiles_per_seq, n_tiles=n_tiles, final_norm=final_norm)
    mixer_tile = lambda g: jnp.minimum(g, n_tiles - 1)
    seq_spec = lambda *tail: pl.BlockSpec(
        (1,) + tail, lambda g: (mixer_tile(g) // tiles_per_seq,) + (0,) * len(tail))
    state_specs = [seq_spec(CONV_WIDTH - 1, CONV_DIM), seq_spec(HEADS, HEAD_DIM, HEAD_DIM),
                   seq_spec(HEADS, HEAD_DIM), seq_spec(8, LANES)]
    ls = lambda arr: _layer_spec(arr, layer)
    y, conv, c, n, m = pl.pallas_call(
        kern,
        grid=(n_tiles + 1,),
        in_specs=[pl.BlockSpec((tt, D_MODEL), lambda g: (mixer_tile(g), 0)),
                  ls(norm1_g), ls(w_nk), ls(w_t), ls(gate_bias), ls(conv_w), ls(hng),
                  _whole_spec(upper), ls(w_out),
                  ls(norm2_g), ls(w_gate), ls(w_up), ls(w_down), _whole_spec(final_g)],
        out_specs=[pl.BlockSpec((tt, D_MODEL), lambda g: (jnp.maximum(g - 1, 0), 0))] + state_specs,
        out_shape=(
            jax.ShapeDtypeStruct((bsz * t_len, D_MODEL), F32),
            jax.ShapeDtypeStruct((bsz, CONV_WIDTH - 1, CONV_DIM), F32),
            jax.ShapeDtypeStruct((bsz, HEADS, HEAD_DIM, HEAD_DIM), F32),
            jax.ShapeDtypeStruct((bsz, HEADS, HEAD_DIM), F32),
            jax.ShapeDtypeStruct((bsz, 8, LANES), F32),
        ),
        scratch_shapes=_mixer_scratch(tt) + [
            pltpu.VMEM((tt, D_MODEL), F32),
            pltpu.VMEM((tt, D_MODEL), BF16),
            pltpu.VMEM((CONV_WIDTH - 1, CONV_DIM), F32),
            pltpu.VMEM((HEADS, STATE_ROWS, HEAD_DIM), F32),
            pltpu.VMEM((8, LANES), F32),
        ],
        compiler_params=pltpu.CompilerParams(
            dimension_semantics=("arbitrary",), vmem_limit_bytes=VMEM_LIMIT),
        name=name,
    )(x.reshape(bsz * t_len, D_MODEL), norm1_g, w_nk, w_t, gate_bias, conv_w, hng, upper, w_out,
      norm2_g, w_gate, w_up, w_down, final_g)
    return y.reshape(bsz, t_len, D_MODEL), conv, c, n, m[:, 0:HEADS, 0]


def _rows_dot_exact(val, mat_ref):
    p1 = val.astype(BF16).astype(F32)
    p2 = (val - p1).astype(BF16).astype(F32)
    p3 = val - p1 - p2
    pieces = jnp.concatenate([p1, p2, p3, jnp.zeros_like(p1)], axis=0).astype(BF16)
    out = _dot(pieces, mat_ref[...])
    return out[0:8] + out[8:16] + out[16:24]


def _sample_layer_kernel(x_ref, g1_ref, wnk_ref, wt_ref, gbias_ref, convw_ref, hng_ref, segut_ref,
                         selend_ref, wout_ref, g2_ref, wg_ref, wu_ref, wd_ref, gf_ref,
                         prev1_ref, prev2_ref, c0_ref, n0_ref, m0_ref,
                         y_ref, z_ref, c_ref, n_ref, mt_ref,
                         hb_ref, pc_ref, k_ref, qt_ref, vt_ref, ot_ref, mix_ref, hb2_ref, cn0_ref,
                         *, n_seq, seg, row_chunk, final_norm):
    rows_all = n_seq * seg
    shift_bits = seg.bit_length() - 1
    n_lane_tiles = rows_all // LANES
    hb_ref[...] = _rms_norm(x_ref[...], g1_ref[...]).astype(BF16)
    for h in range(HEADS):
        for s in range(n_seq):
            r0 = s * STATE_ROWS
            cn0_ref[h, r0:r0 + HEAD_DIM, :] = c0_ref[s, h].T
            cn0_ref[h, r0 + HEAD_DIM:r0 + STATE_ROWS, :] = jnp.broadcast_to(
                n0_ref[s, h:h + 1, :], (STATE_ROWS - HEAD_DIM, HEAD_DIM))

    gq = lax.dot_general(wt_ref[0:ROW_V, :], hb_ref[...], _NT, preferred_element_type=F32)
    qt_ref[...] = gq[GATE_ROWS:, :].astype(BF16)
    graw = gq[0:GATE_ROWS, :] + jnp.concatenate([gbias_ref[...]] * n_lane_tiles, axis=1)
    li = graw[0:8]
    lf = _log_sigmoid(graw[8:16])
    bt = _rows_dot_exact(lf, segut_ref)
    a = li - bt
    frame = lax.broadcasted_iota(jnp.int32, (8, rows_all), 1) & (seg - 1)
    cmax = a
    shift = 1
    while shift < seg:
        cmax = jnp.maximum(cmax, jnp.where(frame >= shift, pltpu.roll(cmax, shift, axis=1), -jnp.inf))
        shift *= 2
    m0 = m0_ref[...]
    mt = bt + jnp.maximum(m0, cmax)
    mt_ref[...] = mt
    u = bt - mt
    eneg = jnp.exp(-mt)
    u_end = _rows_dot_exact(u, selend_ref)
    w_carry = jnp.exp(u + m0)
    w_state = jnp.exp(a + u_end)
    decay = jnp.exp(u_end + m0)
    pad_rows = jnp.zeros((LANES - 8, rows_all), F32)
    a_cols = jnp.concatenate([a, pad_rows], axis=0).T
    decay_cols = jnp.concatenate([decay, pad_rows], axis=0).T

    pc_ref[...] = _dot(hb_ref[...], wnk_ref[:, 0:3 * CONV_DIM])
    w0 = convw_ref[0:1, :]
    w1 = convw_ref[1:2, :]
    w2 = convw_ref[2:3, :]
    frame_c = lax.broadcasted_iota(jnp.int32, (row_chunk, CONV_DIM), 0) & (seg - 1)
    for r0 in range(0, rows_all, row_chunk):
        rows = slice(r0, r0 + row_chunk)
        z = pc_ref[rows, 2 * CONV_DIM:3 * CONV_DIM] * pc_ref[rows, 0:CONV_DIM]
        z_ref[rows, :] = z
        z1 = jnp.where(frame_c == 0, prev1_ref[rows, :], pltpu.roll(z, 1, axis=0))
        z2 = jnp.where(frame_c == 0, prev2_ref[rows, :],
                       jnp.where(frame_c == 1, prev1_ref[rows, :], pltpu.roll(z, 2, axis=0)))
        y = z2 * w0 + z1 * w1 + z * w2
        mix_ref[rows, 0:CONV_DIM] = (pc_ref[rows, CONV_DIM:2 * CONV_DIM] * y).astype(BF16)

    k_ref[...] = (_dot(hb_ref[...], wnk_ref[:, 3 * CONV_DIM:3 * CONV_DIM + MLSTM_DIM])
                  * HEAD_DIM ** -0.5).astype(BF16)
    vt = lax.dot_general(wt_ref[ROW_V:ROW_O, :], hb_ref[...], _NT, preferred_element_type=F32)
    for h in range(HEADS):
        vt_ref[h, 0:HEAD_DIM, :] = vt[h * HEAD_DIM:(h + 1) * HEAD_DIM, :].astype(BF16)
        vt_ref[h, HEAD_DIM:STATE_ROWS, :] = jnp.ones((STATE_ROWS - HEAD_DIM, rows_all), BF16)
    ot_ref[...] = _sigmoid(lax.dot_general(wt_ref[ROW_O:ROW_O + MLSTM_DIM, :], hb_ref[...], _NT,
                                           preferred_element_type=F32))

    row_i = lax.broadcasted_iota(jnp.int32, (rows_all, rows_all), 0)
    col_i = lax.broadcasted_iota(jnp.int32, (rows_all, rows_all), 1)
    causal_t = jnp.logical_and((row_i >> shift_bits) == (col_i >> shift_bits), row_i <= col_i)
    lane_seq = lax.broadcasted_iota(jnp.int32, (STATE_ROWS, rows_all), 1) >> shift_bits
    for h in range(HEADS):
        hcols = slice(h * HEAD_DIM, (h + 1) * HEAD_DIM)
        k = k_ref[:, hcols]
        q_t = qt_ref[hcols, :]
        v_t = vt_ref[h]
        w_t = jnp.exp(jnp.where(causal_t, a_cols[:, h:h + 1] + u[h:h + 1, :], -jnp.inf))
        s_t = (_dot(k, q_t) * w_t).astype(BF16)
        carried = _dot(cn0_ref[h].astype(BF16), q_t)
        own = jnp.zeros((STATE_ROWS, rows_all), F32)
        for s in range(n_seq):
            own = own + jnp.where(lane_seq == s, carried[s * STATE_ROWS:(s + 1) * STATE_ROWS], 0.0)
        tot = w_carry[h:h + 1, :] * own + _dot(v_t, s_t)
        den = tot[HEAD_DIM:HEAD_DIM + 1, :]
        inv = 1.0 / jnp.maximum(jnp.abs(den), eneg[h:h + 1, :])
        h_t = tot[0:HEAD_DIM, :] * inv
        mu = jnp.mean(h_t, axis=0, keepdims=True)
        d = h_t - mu
        var = jnp.mean(d * d, axis=0, keepdims=True)
        g = jnp.concatenate([hng_ref[hcols, :]] * n_lane_tiles, axis=1)
        y_t = d * lax.rsqrt(var + EPS) * g * ot_ref[hcols, :]
        mix_ref[:, CONV_DIM + h * HEAD_DIM:CONV_DIM + (h + 1) * HEAD_DIM] = y_t.T.astype(BF16)
        v_w = v_t.astype(F32) * w_state[h:h + 1, :]
        v_w_streams = jnp.concatenate(
            [jnp.where(lane_seq == s, v_w, 0.0).astype(BF16) for s in range(n_seq)], axis=0)
        update = _dot(v_w_streams, k)
        for s in range(n_seq):
            block = slice(s * STATE_ROWS, (s + 1) * STATE_ROWS)
            dec = jnp.broadcast_to(decay_cols[s * seg:s * seg + 1, h:h + 1], (STATE_ROWS, HEAD_DIM))
            cn_new = dec * cn0_ref[h, block, :] + update[block]
            c_ref[s, h] = cn_new[0:HEAD_DIM].T
            n_ref[s, h:h + 1, :] = cn_new[HEAD_DIM:HEAD_DIM + 1]

    xo = x_ref[...] + _dot(mix_ref[...], wout_ref[...])
    _ffn_norm(xo, g2_ref, hb2_ref)
    y_ref[...] = xo
    _ffn_matmuls(y_ref, hb2_ref, wg_ref, wu_ref, wd_ref, gf_ref, y_ref, final_norm=final_norm)


def _sample_layer(x, mixer_w, ffn_w, conv0, c0, n0, m0, *, layer, final_norm, name):
    bsz, seg, _ = x.shape
    rows = bsz * seg
    assert seg & (seg - 1) == 0 and seg >= CONV_WIDTH - 1 and rows % LANES == 0
    norm1_g, w_nk, w_t, gate_bias, conv_w, hng, w_out = mixer_w
    norm2_g, w_gate, w_up, w_down, final_g = ffn_w
    s_i = lax.broadcasted_iota(jnp.int32, (rows, rows), 0)
    t_i = lax.broadcasted_iota(jnp.int32, (rows, rows), 1)
    seg_upper = jnp.logical_and(s_i // seg == t_i // seg, s_i <= t_i).astype(BF16)
    sel_end = (s_i == (t_i // seg) * seg + seg - 1).astype(BF16)
    prev1 = jnp.repeat(conv0[:, 1, :], seg, axis=0)
    prev2 = jnp.repeat(conv0[:, 0, :], seg, axis=0)
    m0_rows = jnp.repeat(jnp.pad(m0.T, ((0, 8 - HEADS), (0, 0))), seg, axis=1)
    ls = lambda arr: _layer_spec(arr, layer)
    whole = lambda shape: pl.BlockSpec(shape, lambda i: (0,) * len(shape))
    kern = functools.partial(_sample_layer_kernel, n_seq=bsz, seg=seg, row_chunk=min(64, rows),
                             final_norm=final_norm)
    y, z, c_new, n_new, mt = pl.pallas_call(
        kern,
        grid=(1,),
        in_specs=[whole((rows, D_MODEL)), ls(norm1_g), ls(w_nk), ls(w_t), ls(gate_bias), ls(conv_w),
                  ls(hng), _whole_spec(seg_upper), _whole_spec(sel_end), ls(w_out),
                  ls(norm2_g), ls(w_gate), ls(w_up), ls(w_down), _whole_spec(final_g),
                  whole(prev1.shape), whole(prev2.shape), whole(c0.shape), whole(n0.shape),
                  whole(m0_rows.shape)],
        out_specs=[whole((rows, D_MODEL)), whole((rows, CONV_DIM)), whole(c0.shape), whole(n0.shape),
                   whole(m0_rows.shape)],
        out_shape=(
            jax.ShapeDtypeStruct((rows, D_MODEL), F32),
            jax.ShapeDtypeStruct((rows, CONV_DIM), F32),
            jax.ShapeDtypeStruct(c0.shape, F32),
            jax.ShapeDtypeStruct(n0.shape, F32),
            jax.ShapeDtypeStruct(m0_rows.shape, F32),
        ),
        scratch_shapes=_mixer_scratch(rows) + [
            pltpu.VMEM((rows, D_MODEL), BF16),
            pltpu.VMEM((HEADS, bsz * STATE_ROWS, HEAD_DIM), F32),
        ],
        compiler_params=pltpu.CompilerParams(
            dimension_semantics=("arbitrary",), vmem_limit_bytes=VMEM_LIMIT),
        name=name,
    )(x.reshape(rows, D_MODEL), norm1_g, w_nk, w_t, gate_bias, conv_w, hng, seg_upper, sel_end, w_out,
      norm2_g, w_gate, w_up, w_down, final_g, prev1, prev2, c0, n0, m0_rows)
    conv = z.reshape(bsz, seg, CONV_DIM)[:, seg - (CONV_WIDTH - 1):, :]
    m_new = mt[0:HEADS, seg - 1::seg].T
    return y.reshape(bsz, seg, D_MODEL), conv, c_new, n_new, m_new


PROMPT_TILE = 512
PROMPT_CHUNK = 512


def kernel(x_prompt, x_sample, state_conv, state_mlstm_C, state_mlstm_n, state_mlstm_m, norm1_g, w_mix_in, conv_w, b_igate, b_fgate, head_norm_g, w_mix_out, norm2_g, w_gate, w_up, w_down, final_norm_g):
    depth = w_mix_in.shape[0]
    bp = x_prompt.shape[0]
    bs, ts, _ = x_sample.shape

    hp = x_prompt
    hs = x_sample
    final_g = final_norm_g.reshape(1, D_MODEL)

    c_q = 3 * CONV_DIM
    c_k, c_v, c_o, c_g = c_q + MLSTM_DIM, c_q + 2 * MLSTM_DIM, c_q + 3 * MLSTM_DIM, c_q + 4 * MLSTM_DIM

    w = w_mix_in
    w_nk = jnp.concatenate([w[:, :, 0:c_q], w[:, :, c_k:c_v]], axis=2).astype(BF16)
    gates = w[:, :, c_g:c_g + 2 * HEADS]
    no_cols = jnp.zeros((depth, D_MODEL, 8 - HEADS), F32)
    gate_cols = jnp.concatenate([gates[:, :, 0:HEADS], no_cols, gates[:, :, HEADS:], no_cols], axis=2)
    transposed = lambda cols: jnp.swapaxes(cols.astype(BF16), 1, 2)
    w_t = jnp.concatenate([transposed(gate_cols), transposed(w[:, :, c_q:c_k]),
                           transposed(w[:, :, c_v:c_o]), transposed(w[:, :, c_o:c_g])], axis=1)
    gate_bias = jnp.zeros((depth, GATE_ROWS), F32)
    gate_bias = gate_bias.at[:, 0:HEADS].set(b_igate).at[:, 8:8 + HEADS].set(b_fgate)
    gate_bias = jnp.broadcast_to(gate_bias[:, :, None], (depth, GATE_ROWS, LANES))
    hng = jnp.broadcast_to(head_norm_g.reshape(depth, MLSTM_DIM, 1), (depth, MLSTM_DIM, LANES))
    mixer_w = (norm1_g.reshape(depth, 1, D_MODEL), w_nk, w_t, gate_bias, conv_w, hng,
               w_mix_out.astype(BF16))
    ffn_w = (norm2_g.reshape(depth, 1, D_MODEL), w_gate.astype(BF16), w_up.astype(BF16),
             w_down.astype(BF16), final_g)

    outs_p, outs_s = [], []
    for l in range(depth):
        last = l == depth - 1
        hp, *state_p = _prompt_layer(
            hp, mixer_w, ffn_w, layer=l, tt=PROMPT_TILE, chunk=PROMPT_CHUNK,
            final_norm=last, name=f"layer_prompt_{l}")
        outs_p.append(state_p)

        hs, *state_s = _sample_layer(
            hs, mixer_w, ffn_w, state_conv[l], state_mlstm_C[l], state_mlstm_n[l], state_mlstm_m[l],
            layer=l, final_norm=last, name=f"layer_sample_{l}")
        outs_s.append(state_s)

    stacked = lambda outs: tuple(jnp.stack([o[i] for o in outs]) for i in range(4))
    return (hp, hs) + stacked(outs_p) + stacked(outs_s)
```

```python
import functools

import jax
import jax.numpy as jnp
from jax import lax
from jax.experimental import pallas as pl
from jax.experimental.pallas import tpu as pltpu

D_MODEL = 1024
CONV_DIM = 512
CONV_WIDTH = 3
HEADS = 4
HEAD_DIM = 128
MLSTM_DIM = HEADS * HEAD_DIM
MIX_DIM = CONV_DIM + MLSTM_DIM
D_FF = 2816
EPS = 1e-6

LANES = 128
BF16_ROWS = 16
MXU_WIDTH = 256
STATE_ROWS = HEAD_DIM + BF16_ROWS
GATE_ROWS = 2 * 8
ROW_V = GATE_ROWS + MLSTM_DIM
ROW_O = ROW_V + MLSTM_DIM
FFN_SPLIT = (D_FF // MXU_WIDTH + 1) // 2 * MXU_WIDTH
VMEM_LIMIT = 60 * 1024 * 1024

BF16 = jnp.bfloat16
F32 = jnp.float32
_NT = (((1,), (1,)), ((), ()))


def _rms_norm(x, g):
    ms = jnp.mean(x * x, axis=-1, keepdims=True)
    return x * lax.rsqrt(ms + EPS) * g


def _sigmoid(x):
    return 1.0 / (1.0 + jnp.exp(-x))


def _log_sigmoid(x):
    return jnp.minimum(x, 0.0) - jnp.log1p(jnp.exp(-jnp.abs(x)))


def _dot(a, b):
    return jnp.dot(a, b, preferred_element_type=F32)


def _mixer_tile(x_ref, w, state, scr, xo_ref, *, tt, chunk, t_valid, row_chunk):
    g1_ref, wnk_ref, wt_ref, gbias_ref, convw_ref, hng_ref, ut_ref, wout_ref = w
    conv_prev, cn_prev, m_start = state
    hb_ref, pc_ref, k_ref, qt_ref, vt_ref, ot_ref, mix_ref = scr

    hb_ref[...] = _rms_norm(x_ref[...], g1_ref[...]).astype(BF16)
    n_lane_tiles = tt // LANES

    gq = lax.dot_general(wt_ref[0:ROW_V, :], hb_ref[...], _NT, preferred_element_type=F32)
    qt_ref[...] = gq[GATE_ROWS:, :].astype(BF16)
    graw = gq[0:GATE_ROWS, :] + jnp.concatenate([gbias_ref[...]] * n_lane_tiles, axis=1)
    li = graw[0:8]
    lf = _log_sigmoid(graw[8:16])
    p1 = lf.astype(BF16).astype(F32)
    p2 = (lf - p1).astype(BF16).astype(F32)
    p3 = lf - p1 - p2
    pieces = jnp.concatenate([p1, p2, p3, jnp.zeros_like(p1)], axis=0).astype(BF16)
    csum = _dot(pieces, ut_ref[...])
    bt = csum[0:8] + csum[8:16] + csum[16:24]
    a = li - bt
    lane_t = lax.broadcasted_iota(jnp.int32, (8, tt), 1)
    cmax = a
    shift = 1
    while shift < tt:
        cmax = jnp.maximum(cmax, jnp.where(lane_t >= shift, pltpu.roll(cmax, shift, axis=1),
                                           -jnp.inf))
        shift *= 2
    mt = bt + jnp.maximum(jnp.concatenate([m_start] * n_lane_tiles, axis=1), cmax)
    u = bt - mt
    eneg = jnp.exp(-mt)
    a_cols = jnp.concatenate([a, jnp.zeros((LANES - 8, tt), F32)], axis=0).T
    m_new = jnp.broadcast_to(mt[:, t_valid - 1:t_valid], (8, LANES))

    pc_ref[...] = _dot(hb_ref[...], wnk_ref[:, 0:3 * CONV_DIM])
    w0 = convw_ref[0:1, :]
    w1 = convw_ref[1:2, :]
    w2 = convw_ref[2:3, :]
    row_c = lax.broadcasted_iota(jnp.int32, (row_chunk, CONV_DIM), 0)
    prev2, prev1 = conv_prev[0:1, :], conv_prev[1:2, :]
    for r0 in range(0, tt, row_chunk):
        rows = slice(r0, r0 + row_chunk)
        z = pc_ref[rows, 2 * CONV_DIM:3 * CONV_DIM] * pc_ref[rows, 0:CONV_DIM]
        z1 = jnp.where(row_c == 0, prev1, pltpu.roll(z, 1, axis=0))
        z2 = jnp.where(row_c == 0, prev2, jnp.where(row_c == 1, prev1, pltpu.roll(z, 2, axis=0)))
        y = z2 * w0 + z1 * w1 + z * w2
        mix_ref[rows, 0:CONV_DIM] = (pc_ref[rows, CONV_DIM:2 * CONV_DIM] * y).astype(BF16)
        prev2, prev1 = z[row_chunk - 2:row_chunk - 1, :], z[row_chunk - 1:row_chunk, :]
        if r0 <= t_valid - 2 < r0 + row_chunk:
            conv_new = z[t_valid - 2 - r0:t_valid - r0, :]

    k_ref[...] = (_dot(hb_ref[...], wnk_ref[:, 3 * CONV_DIM:3 * CONV_DIM + MLSTM_DIM])
                  * HEAD_DIM ** -0.5).astype(BF16)
    vt = lax.dot_general(wt_ref[ROW_V:ROW_O, :], hb_ref[...], _NT, preferred_element_type=F32)
    for h in range(HEADS):
        vt_ref[h, 0:HEAD_DIM, :] = vt[h * HEAD_DIM:(h + 1) * HEAD_DIM, :].astype(BF16)
        vt_ref[h, HEAD_DIM:STATE_ROWS, :] = jnp.ones((STATE_ROWS - HEAD_DIM, tt), BF16)
    ot_ref[...] = _sigmoid(lax.dot_general(wt_ref[ROW_O:ROW_O + MLSTM_DIM, :], hb_ref[...], _NT,
                                           preferred_element_type=F32))

    causal_t = (lax.broadcasted_iota(jnp.int32, (chunk, chunk), 0)
                <= lax.broadcasted_iota(jnp.int32, (chunk, chunk), 1))
    lane_c = lax.broadcasted_iota(jnp.int32, (8, chunk), 1)
    c_lane_tiles = chunk // LANES

    cn_cur = list(cn_prev)
    for c in range(tt // chunk):
        cs = c * chunk
        lv = min(chunk, t_valid - cs)
        if lv <= 0:
            break
        ce = cs + lv - 1
        rows = slice(cs, cs + chunk)
        if c == 0:
            kappa = jnp.concatenate([m_start] * c_lane_tiles, axis=1)
        else:
            kappa = jnp.broadcast_to(-u[:, cs - 1:cs], (8, chunk))
        u_end = u[:, ce:ce + 1]
        w_carry = jnp.exp(u[:, rows] + kappa)
        w_state = jnp.exp(a[:, rows] + jnp.broadcast_to(u_end, (8, chunk)))
        if lv < chunk:
            w_state = jnp.where(lane_c < lv, w_state, 0.0)
        decay = jnp.exp(jnp.broadcast_to(u_end, (8, LANES)) + kappa[:, 0:LANES])
        for h in range(HEADS):
            hcols = slice(h * HEAD_DIM, (h + 1) * HEAD_DIM)
            k = k_ref[rows, hcols]
            q_t = qt_ref[hcols, rows]
            v_t = vt_ref[h, :, rows]
            w_t = jnp.exp(jnp.where(causal_t, a_cols[rows, h:h + 1] + u[h:h + 1, rows], -jnp.inf))
            s_t = (_dot(k, q_t) * w_t).astype(BF16)
            cn = cn_cur[h]
            tot = w_carry[h:h + 1, :] * _dot(cn.astype(BF16), q_t) + _dot(v_t, s_t)
            den = tot[HEAD_DIM:HEAD_DIM + 1, :]
            inv = 1.0 / jnp.maximum(jnp.abs(den), eneg[h:h + 1, rows])
            h_t = tot[0:HEAD_DIM, :] * inv
            mu = jnp.mean(h_t, axis=0, keepdims=True)
            d = h_t - mu
            var = jnp.mean(d * d, axis=0, keepdims=True)
            g = jnp.concatenate([hng_ref[hcols, :]] * c_lane_tiles, axis=1)
            y_t = d * lax.rsqrt(var + EPS) * g * ot_ref[hcols, rows]
            mix_ref[rows, CONV_DIM + h * HEAD_DIM:CONV_DIM + (h + 1) * HEAD_DIM] = (
                y_t.T.astype(BF16))
            v_w = (v_t.astype(F32) * w_state[h:h + 1, :]).astype(BF16)
            cn_cur[h] = decay[h:h + 1, :] * cn + _dot(v_w, k)

    xo = x_ref[...] + _dot(mix_ref[...], wout_ref[...])
    xo_ref[...] = xo
    return conv_new, cn_cur, m_new, xo


def _ffn_norm(x, g2_ref, hb_ref):
    hb_ref[...] = _rms_norm(x, g2_ref[...]).astype(BF16)


def _ffn_matmuls(x_ref, hb_ref, wg_ref, wu_ref, wd_ref, gf_ref, y_ref, *, final_norm):
    parts = ((0, FFN_SPLIT), (FFN_SPLIT, D_FF))
    for i, (f0, f1) in enumerate(parts):
        g = _dot(hb_ref[...], wg_ref[:, f0:f1])
        u = _dot(hb_ref[...], wu_ref[:, f0:f1])
        a = (g * _sigmoid(g) * u).astype(BF16)
        base = x_ref if i == 0 else y_ref
        acc = base[...] + _dot(a, wd_ref[f0:f1, :])
        if final_norm and i == len(parts) - 1:
            acc = _rms_norm(acc, gf_ref[...])
        y_ref[...] = acc


def _mixer_scratch(tt):
    return [
        pltpu.VMEM((tt, D_MODEL), BF16),
        pltpu.VMEM((tt, 3 * CONV_DIM), F32),
        pltpu.VMEM((tt, MLSTM_DIM), BF16),
        pltpu.VMEM((MLSTM_DIM, tt), BF16),
        pltpu.VMEM((HEADS, STATE_ROWS, tt), BF16),
        pltpu.VMEM((MLSTM_DIM, tt), F32),
        pltpu.VMEM((tt, MIX_DIM), BF16),
    ]


def _upper_ones(tt):
    return (lax.broadcasted_iota(jnp.int32, (tt, tt), 0)
            <= lax.broadcasted_iota(jnp.int32, (tt, tt), 1)).astype(BF16)


def _layer_spec(stacked, layer):
    tail = stacked.shape[1:]
    return pl.BlockSpec((None,) + tail, lambda *_: (layer,) + (0,) * len(tail),
                        pipeline_mode=pl.Buffered(1))


def _whole_spec(arr):
    return pl.BlockSpec(arr.shape, lambda *_: (0,) * arr.ndim, pipeline_mode=pl.Buffered(1))


def _layer_kernel(x_ref, g1_ref, wnk_ref, wt_ref, gbias_ref, convw_ref, hng_ref, ut_ref, wout_ref,
                  g2_ref, wg_ref, wu_ref, wd_ref, gf_ref,
                  y_ref, conv_ref, c_ref, n_ref, m_ref,
                  hb_ref, pc_ref, k_ref, qt_ref, vt_ref, ot_ref, mix_ref, x1_ref, hb2_ref,
                  conv_s, cn_s, m_s,
                  *, tt, chunk, row_chunk, tiles_per_seq, n_tiles, final_norm):
    g = pl.program_id(0)
    tile_in_seq = lax.rem(g, tiles_per_seq)

    @pl.when(g == 0)
    def _():
        x1_ref[...] = jnp.zeros_like(x1_ref)

    @pl.when(tile_in_seq == 0)
    def _():
        conv_s[...] = jnp.zeros_like(conv_s)
        cn_s[...] = jnp.zeros_like(cn_s)
        m_s[...] = jnp.zeros_like(m_s)

    x1 = x1_ref[...]
    _ffn_norm(x1, g2_ref, hb2_ref)
    y_ref[...] = x1

    state = (conv_s[...], [cn_s[h] for h in range(HEADS)], m_s[...])
    conv_new, cn_new, m_new, _ = _mixer_tile(
        x_ref, (g1_ref, wnk_ref, wt_ref, gbias_ref, convw_ref, hng_ref, ut_ref, wout_ref), state,
        (hb_ref, pc_ref, k_ref, qt_ref, vt_ref, ot_ref, mix_ref), x1_ref,
        tt=tt, chunk=chunk, t_valid=tt, row_chunk=row_chunk)
    _ffn_matmuls(y_ref, hb2_ref, wg_ref, wu_ref, wd_ref, gf_ref, y_ref, final_norm=final_norm)
    conv_s[...] = conv_new
    for h in range(HEADS):
        cn_s[h] = cn_new[h]
    m_s[...] = m_new

    @pl.when(jnp.logical_and(tile_in_seq == tiles_per_seq - 1, g < n_tiles))
    def _():
        conv_ref[0] = conv_s[...]
        for h in range(HEADS):
            c_ref[0, h] = cn_s[h, 0:HEAD_DIM, :].T
            n_ref[0, h:h + 1, :] = cn_s[h, HEAD_DIM:HEAD_DIM + 1, :]
        m_ref[0] = m_s[...]


def _prompt_layer(x, mixer_w, ffn_w, *, layer, tt, chunk, final_norm, name):
    bsz, t_len, _ = x.shape
    tiles_per_seq = t_len // tt
    n_tiles = bsz * tiles_per_seq
    norm1_g, w_nk, w_t, gate_bias, conv_w, hng, w_out = mixer_w
    norm2_g, w_gate, w_up, w_down, final_g = ffn_w
    upper = _upper_ones(tt)
    kern = functools.partial(_layer_kernel, tt=tt, chunk=chunk, row_chunk=min(64, tt),
                             tiles_per_seq=tiles_per_seq, n_tiles=n_tiles, final_norm=final_norm)
    mixer_tile = lambda g: jnp.minimum(g, n_tiles - 1)
    seq_spec = lambda *tail: pl.BlockSpec(
        (1,) + tail, lambda g: (mixer_tile(g) // tiles_per_seq,) + (0,) * len(tail))
    state_specs = [seq_spec(CONV_WIDTH - 1, CONV_DIM), seq_spec(HEADS, HEAD_DIM, HEAD_DIM),
                   seq_spec(HEADS, HEAD_DIM), seq_spec(8, LANES)]
    ls = lambda arr: _layer_spec(arr, layer)
    y, conv, c, n, m = pl.pallas_call(
        kern,
        grid=(n_tiles + 1,),
        in_specs=[pl.BlockSpec((tt, D_MODEL), lambda g: (mixer_tile(g), 0)),
                  ls(norm1_g), ls(w_nk), ls(w_t), ls(gate_bias), ls(conv_w), ls(hng),
                  _whole_spec(upper), ls(w_out),
                  ls(norm2_g), ls(w_gate), ls(w_up), ls(w_down), _whole_spec(final_g)],
        out_specs=[pl.BlockSpec((tt, D_MODEL), lambda g: (jnp.maximum(g - 1, 0), 0))] + state_specs,
        out_shape=(
            jax.ShapeDtypeStruct((bsz * t_len, D_MODEL), F32),
            jax.ShapeDtypeStruct((bsz, CONV_WIDTH - 1, CONV_DIM), F32),
            jax.ShapeDtypeStruct((bsz, HEADS, HEAD_DIM, HEAD_DIM), F32),
            jax.ShapeDtypeStruct((bsz, HEADS, HEAD_DIM), F32),
            jax.ShapeDtypeStruct((bsz, 8, LANES), F32),
        ),
        scratch_shapes=_mixer_scratch(tt) + [
            pltpu.VMEM((tt, D_MODEL), F32),
            pltpu.VMEM((tt, D_MODEL), BF16),
            pltpu.VMEM((CONV_WIDTH - 1, CONV_DIM), F32),
            pltpu.VMEM((HEADS, STATE_ROWS, HEAD_DIM), F32),
            pltpu.VMEM((8, LANES), F32),
        ],
        compiler_params=pltpu.CompilerParams(
            dimension_semantics=("arbitrary",), vmem_limit_bytes=VMEM_LIMIT),
        name=name,
    )(x.reshape(bsz * t_len, D_MODEL), norm1_g, w_nk, w_t, gate_bias, conv_w, hng, upper, w_out,
      norm2_g, w_gate, w_up, w_down, final_g)
    return y.reshape(bsz, t_len, D_MODEL), conv, c, n, m[:, 0:HEADS, 0]


def _rows_dot_exact(val, mat_ref):
    p1 = val.astype(BF16).astype(F32)
    p2 = (val - p1).astype(BF16).astype(F32)
    p3 = val - p1 - p2
    pieces = jnp.concatenate([p1, p2, p3, jnp.zeros_like(p1)], axis=0).astype(BF16)
    out = _dot(pieces, mat_ref[...])
    return out[0:8] + out[8:16] + out[16:24]


def _sample_layer_kernel(x_ref, g1_ref, wnk_ref, wt_ref, gbias_ref, convw_ref, hng_ref, segut_ref,
                         selend_ref, wout_ref, g2_ref, wg_ref, wu_ref, wd_ref, gf_ref,
                         prev1_ref, prev2_ref, c0_ref, n0_ref, m0_ref,
                         y_ref, z_ref, c_ref, n_ref, mt_ref,
                         hb_ref, pc_ref, k_ref, qt_ref, vt_ref, ot_ref, mix_ref, hb2_ref, cn0_ref,
                         *, n_seq, seg, row_chunk, final_norm):
    rows_all = n_seq * seg
    shift_bits = seg.bit_length() - 1
    n_lane_tiles = rows_all // LANES
    hb_ref[...] = _rms_norm(x_ref[...], g1_ref[...]).astype(BF16)
    for h in range(HEADS):
        for s in range(n_seq):
            r0 = s * STATE_ROWS
            cn0_ref[h, r0:r0 + HEAD_DIM, :] = c0_ref[s, h].T
            cn0_ref[h, r0 + HEAD_DIM:r0 + STATE_ROWS, :] = jnp.broadcast_to(
                n0_ref[s, h:h + 1, :], (STATE_ROWS - HEAD_DIM, HEAD_DIM))

    gq = lax.dot_general(wt_ref[0:ROW_V, :], hb_ref[...], _NT, preferred_element_type=F32)
    qt_ref[...] = gq[GATE_ROWS:, :].astype(BF16)
    graw = gq[0:GATE_ROWS, :] + jnp.concatenate([gbias_ref[...]] * n_lane_tiles, axis=1)
    li = graw[0:8]
    lf = _log_sigmoid(graw[8:16])
    bt = _rows_dot_exact(lf, segut_ref)
    a = li - bt
    frame = lax.broadcasted_iota(jnp.int32, (8, rows_all), 1) & (seg - 1)
    cmax = a
    shift = 1
    while shift < seg:
        cmax = jnp.maximum(cmax, jnp.where(frame >= shift, pltpu.roll(cmax, shift, axis=1), -jnp.inf))
        shift *= 2
    m0 = m0_ref[...]
    mt = bt + jnp.maximum(m0, cmax)
    mt_ref[...] = mt
    u = bt - mt
    eneg = jnp.exp(-mt)
    u_end = _rows_dot_exact(u, selend_ref)
    w_carry = jnp.exp(u + m0)
    w_state = jnp.exp(a + u_end)
    decay = jnp.exp(u_end + m0)
    pad_rows = jnp.zeros((LANES - 8, rows_all), F32)
    a_cols = jnp.concatenate([a, pad_rows], axis=0).T
    decay_cols = jnp.concatenate([decay, pad_rows], axis=0).T

    pc_ref[...] = _dot(hb_ref[...], wnk_ref[:, 0:3 * CONV_DIM])
    w0 = convw_ref[0:1, :]
    w1 = convw_ref[1:2, :]
    w2 = convw_ref[2:3, :]
    frame_c = lax.broadcasted_iota(jnp.int32, (row_chunk, CONV_DIM), 0) & (seg - 1)
    for r0 in range(0, rows_all, row_chunk):
        rows = slice(r0, r0 + row_chunk)
        z = pc_ref[rows, 2 * CONV_DIM:3 * CONV_DIM] * pc_ref[rows, 0:CONV_DIM]
        z_ref[rows, :] = z
        z1 = jnp.where(frame_c == 0, prev1_ref[rows, :], pltpu.roll(z, 1, axis=0))
        z2 = jnp.where(frame_c == 0, prev2_ref[rows, :],
                       jnp.where(frame_c == 1, prev1_ref[rows, :], pltpu.roll(z, 2, axis=0)))
        y = z2 * w0 + z1 * w1 + z * w2
        mix_ref[rows, 0:CONV_DIM] = (pc_ref[rows, CONV_DIM:2 * CONV_DIM] * y).astype(BF16)

    k_ref[...] = (_dot(hb_ref[...], wnk_ref[:, 3 * CONV_DIM:3 * CONV_DIM + MLSTM_DIM])
                  * HEAD_DIM ** -0.5).astype(BF16)
    vt = lax.dot_general(wt_ref[ROW_V:ROW_O, :], hb_ref[...], _NT, preferred_element_type=F32)
    for h in range(HEADS):
        vt_ref[h, 0:HEAD_DIM, :] = vt[h * HEAD_DIM:(h + 1) * HEAD_DIM, :].astype(BF16)
        vt_ref[h, HEAD_DIM:STATE_ROWS, :] = jnp.ones((STATE_ROWS - HEAD_DIM, rows_all), BF16)
    ot_ref[...] = _sigmoid(lax.dot_general(wt_ref[ROW_O:ROW_O + MLSTM_DIM, :], hb_ref[...], _NT,
                                           preferred_element_type=F32))

    row_i = lax.broadcasted_iota(jnp.int32, (rows_all, rows_all), 0)
    col_i = lax.broadcasted_iota(jnp.int32, (rows_all, rows_all), 1)
    causal_t = jnp.logical_and((row_i >> shift_bits) == (col_i >> shift_bits), row_i <= col_i)
    lane_seq = lax.broadcasted_iota(jnp.int32, (STATE_ROWS, rows_all), 1) >> shift_bits
    for h in range(HEADS):
        hcols = slice(h * HEAD_DIM, (h + 1) * HEAD_DIM)
        k = k_ref[:, hcols]
        q_t = qt_ref[hcols, :]
        v_t = vt_ref[h]
        w_t = jnp.exp(jnp.where(causal_t, a_cols[:, h:h + 1] + u[h:h + 1, :], -jnp.inf))
        s_t = (_dot(k, q_t) * w_t).astype(BF16)
        carried = _dot(cn0_ref[h].astype(BF16), q_t)
        own = jnp.zeros((STATE_ROWS, rows_all), F32)
        for s in range(n_seq):
            own = own + jnp.where(lane_seq == s, carried[s * STATE_ROWS:(s + 1) * STATE_ROWS], 0.0)
        tot = w_carry[h:h + 1, :] * own + _dot(v_t, s_t)
        den = tot[HEAD_DIM:HEAD_DIM + 1, :]
        inv = 1.0 / jnp.maximum(jnp.abs(den), eneg[h:h + 1, :])
        h_t = tot[0:HEAD_DIM, :] * inv
        mu = jnp.mean(h_t, axis=0, keepdims=True)
        d = h_t - mu
        var = jnp.mean(d * d, axis=0, keepdims=True)
        g = jnp.concatenate([hng_ref[hcols, :]] * n_lane_tiles, axis=1)
        y_t = d * lax.rsqrt(var + EPS) * g * ot_ref[hcols, :]
        mix_ref[:, CONV_DIM + h * HEAD_DIM:CONV_DIM + (h + 1) * HEAD_DIM] = y_t.T.astype(BF16)
        v_w = v_t.astype(F32) * w_state[h:h + 1, :]
        v_w_streams = jnp.concatenate(
            [jnp.where(lane_seq == s, v_w, 0.0).astype(BF16) for s in range(n_seq)], axis=0)
        update = _dot(v_w_streams, k)
        for s in range(n_seq):
            block = slice(s * STATE_ROWS, (s + 1) * STATE_ROWS)
            dec = jnp.broadcast_to(decay_cols[s * seg:s * seg + 1, h:h + 1], (STATE_ROWS, HEAD_DIM))
            cn_new = dec * cn0_ref[h, block, :] + update[block]
            c_ref[s, h] = cn_new[0:HEAD_DIM].T
            n_ref[s, h:h + 1, :] = cn_new[HEAD_DIM:HEAD_DIM + 1]

    xo = x_ref[...] + _dot(mix_ref[...], wout_ref[...])
    _ffn_norm(xo, g2_ref, hb2_ref)
    y_ref[...] = xo
    _ffn_matmuls(y_ref, hb2_ref, wg_ref, wu_ref, wd_ref, gf_ref, y_ref, final_norm=final_norm)


def _sample_layer(x, mixer_w, ffn_w, conv0, c0, n0, m0, *, layer, final_norm, name):
    bsz, seg, _ = x.shape
    rows = bsz * seg
    assert seg & (seg - 1) == 0 and seg >= CONV_WIDTH - 1 and rows % LANES == 0
    norm1_g, w_nk, w_t, gate_bias, conv_w, hng, w_out = mixer_w
    norm2_g, w_gate, w_up, w_down, final_g = ffn_w
    s_i = lax.broadcasted_iota(jnp.int32, (rows, rows), 0)
    t_i = lax.broadcasted_iota(jnp.int32, (rows, rows), 1)
    seg_upper = jnp.logical_and(s_i // seg == t_i // seg, s_i <= t_i).astype(BF16)
    sel_end = (s_i == (t_i // seg) * seg + seg - 1).astype(BF16)
    prev1 = jnp.repeat(conv0[:, 1, :], seg, axis=0)
    prev2 = jnp.repeat(conv0[:, 0, :], seg, axis=0)
    m0_rows = jnp.repeat(jnp.pad(m0.T, ((0, 8 - HEADS), (0, 0))), seg, axis=1)
    ls = lambda arr: _layer_spec(arr, layer)
    whole = lambda shape: pl.BlockSpec(shape, lambda i: (0,) * len(shape))
    kern = functools.partial(_sample_layer_kernel, n_seq=bsz, seg=seg, row_chunk=min(64, rows),
                             final_norm=final_norm)
    y, z, c_new, n_new, mt = pl.pallas_call(
        kern,
        grid=(1,),
        in_specs=[whole((rows, D_MODEL)), ls(norm1_g), ls(w_nk), ls(w_t), ls(gate_bias), ls(conv_w),
                  ls(hng), _whole_spec(seg_upper), _whole_spec(sel_end), ls(w_out),
                  ls(norm2_g), ls(w_gate), ls(w_up), ls(w_down), _whole_spec(final_g),
                  whole(prev1.shape), whole(prev2.shape), whole(c0.shape), whole(n0.shape),
                  whole(m0_rows.shape)],
        out_specs=[whole((rows, D_MODEL)), whole((rows, CONV_DIM)), whole(c0.shape), whole(n0.shape),
                   whole(m0_rows.shape)],
        out_shape=(
            jax.ShapeDtypeStruct((rows, D_MODEL), F32),
            jax.ShapeDtypeStruct((rows, CONV_DIM), F32),
            jax.ShapeDtypeStruct(c0.shape, F32),
            jax.ShapeDtypeStruct(n0.shape, F32),
            jax.ShapeDtypeStruct(m0_rows.shape, F32),
        ),
        scratch_shapes=_mixer_scratch(rows) + [
            pltpu.VMEM((rows, D_MODEL), BF16),
            pltpu.VMEM((HEADS, bsz * STATE_ROWS, HEAD_DIM), F32),
        ],
        compiler_params=pltpu.CompilerParams(
            dimension_semantics=("arbitrary",), vmem_limit_bytes=VMEM_LIMIT),
        name=name,
    )(x.reshape(rows, D_MODEL), norm1_g, w_nk, w_t, gate_bias, conv_w, hng, seg_upper, sel_end, w_out,
      norm2_g, w_gate, w_up, w_down, final_g, prev1, prev2, c0, n0, m0_rows)
    conv = z.reshape(bsz, seg, CONV_DIM)[:, seg - (CONV_WIDTH - 1):, :]
    m_new = mt[0:HEADS, seg - 1::seg].T
    return y.reshape(bsz, seg, D_MODEL), conv, c_new, n_new, m_new


PROMPT_TILE = 512
PROMPT_CHUNK = 512


def kernel(x_prompt, x_sample, state_conv, state_mlstm_C, state_mlstm_n, state_mlstm_m, norm1_g, w_mix_in, conv_w, b_igate, b_fgate, head_norm_g, w_mix_out, norm2_g, w_gate, w_up, w_down, final_norm_g):
    depth = w_mix_in.shape[0]

    hp = x_prompt
    hs = x_sample
    final_g = final_norm_g.reshape(1, D_MODEL)

    c_q = 3 * CONV_DIM
    c_k, c_v, c_o, c_g = c_q + MLSTM_DIM, c_q + 2 * MLSTM_DIM, c_q + 3 * MLSTM_DIM, c_q + 4 * MLSTM_DIM

    w = w_mix_in
    w_nk = jnp.concatenate([w[:, :, 0:c_q], w[:, :, c_k:c_v]], axis=2).astype(BF16)
    gates = w[:, :, c_g:c_g + 2 * HEADS]
    no_cols = jnp.zeros((depth, D_MODEL, 8 - HEADS), F32)
    gate_cols = jnp.concatenate([gates[:, :, 0:HEADS], no_cols, gates[:, :, HEADS:], no_cols], axis=2)
    transposed = lambda cols: jnp.swapaxes(cols.astype(BF16), 1, 2)
    w_t = jnp.concatenate([transposed(gate_cols), transposed(w[:, :, c_q:c_k]),
                           transposed(w[:, :, c_v:c_o]), transposed(w[:, :, c_o:c_g])], axis=1)
    gate_bias = jnp.zeros((depth, GATE_ROWS), F32)
    gate_bias = gate_bias.at[:, 0:HEADS].set(b_igate).at[:, 8:8 + HEADS].set(b_fgate)
    gate_bias = jnp.broadcast_to(gate_bias[:, :, None], (depth, GATE_ROWS, LANES))
    hng = jnp.broadcast_to(head_norm_g.reshape(depth, MLSTM_DIM, 1), (depth, MLSTM_DIM, LANES))
    mixer_w = (norm1_g.reshape(depth, 1, D_MODEL), w_nk, w_t, gate_bias, conv_w, hng,
               w_mix_out.astype(BF16))
    ffn_w = (norm2_g.reshape(depth, 1, D_MODEL), w_gate.astype(BF16), w_up.astype(BF16),
             w_down.astype(BF16), final_g)

    outs_p, outs_s = [], []
    for l in range(depth):
        last = l == depth - 1
        hp, *state_p = _prompt_layer(
            hp, mixer_w, ffn_w, layer=l, tt=PROMPT_TILE, chunk=PROMPT_CHUNK,
            final_norm=last, name=f"layer_prompt_{l}")
        outs_p.append(state_p)

        hs, *state_s = _sample_layer(
            hs, mixer_w, ffn_w, state_conv[l], state_mlstm_C[l], state_mlstm_n[l], state_mlstm_m[l],
            layer=l, final_norm=last, name=f"layer_sample_{l}")
        outs_s.append(state_s)

    stacked = lambda outs: tuple(jnp.stack([o[i] for o in outs]) for i in range(4))
    return (hp, hs) + stacked(outs_p) + stacked(outs_s)
```
